```python
import jax, jax.numpy as jnp
from jax import lax
import numpy as np

D_MODEL = 1024
BATCH = 32
SEQ = 2048
DEPTH = 2

GRID_W = 64
CTX_LEN = 256
MIX_WIDTH = D_MODEL
RET_WIDTH = MIX_WIDTH // 2
RET_HEADS = 8
RET_HEAD_DIM = RET_WIDTH // RET_HEADS
FOURIER_WIDTH = MIX_WIDTH // 4
FOURIER_GROUPS = 4
CONV_WIDTH = MIX_WIDTH // 4
CONV_GROUPS = 4
CHUNK = 128
ROPE_BASE = 10000.0
FFN_HIDDEN = ((8 * D_MODEL // 3 + 127) // 128) * 128
EPS = 1e-6

Q0 = 0
K0 = Q0 + RET_WIDTH
V0 = K0 + RET_WIDTH
G0 = V0 + RET_WIDTH
F0 = G0 + RET_WIDTH
CB0 = F0 + FOURIER_WIDTH
CC0 = CB0 + CONV_WIDTH
CH0 = CC0 + CONV_WIDTH
IN_WIDTH = CH0 + CONV_WIDTH

kernel_name = 'hybrid_retention_fourier_shortconv_dit'


def rmsnorm(x, g=None):
    xf = x.astype(jnp.float32)
    y = xf * lax.rsqrt(jnp.mean(xf * xf, axis=-1, keepdims=True) + EPS)
    if g is not None:
        y = y * g.astype(jnp.float32)
    return y.astype(x.dtype)


def dwconv3(x, w):
    xp = jnp.pad(x, ((0, 0), (1, 1), (0, 0)))
    return xp[:, :-2] * w[0] + xp[:, 1:-1] * w[1] + xp[:, 2:] * w[2]


def axial_rope(T):
    rows = T // GRID_W
    row = jnp.repeat(jnp.arange(rows, dtype=jnp.float32), GRID_W)
    col = jnp.tile(jnp.arange(GRID_W, dtype=jnp.float32), rows)
    n_freq = RET_HEAD_DIM // 4
    freq = ROPE_BASE ** (-jnp.arange(n_freq, dtype=jnp.float32) / n_freq)
    ang = jnp.concatenate([row[:, None] * freq, col[:, None] * freq], axis=-1)
    return jnp.cos(ang)[None, :, None, :], jnp.sin(ang)[None, :, None, :]


def apply_rope(t, cos, sin):
    half = t.shape[-1] // 2
    t1, t2 = t[..., :half], t[..., half:]
    return jnp.concatenate([t1 * cos - t2 * sin, t1 * sin + t2 * cos], axis=-1)


def _to_chunks(t):
    B, T, H, d = t.shape
    return t.reshape(B, T // CHUNK, CHUNK, H, d)


def _state_decay(log_decay):
    i = jnp.arange(CHUNK, dtype=jnp.float32)
    zeta = jnp.exp(log_decay[:, None] * (CHUNK - 1.0 - i))
    return zeta, jnp.exp(log_decay * CHUNK)


def _chunk_states(kc, vc, log_decay, init):
    zeta, chunk_decay = _state_decay(log_decay)
    u = jnp.einsum('bnjhd,hj,bnjhe->nbhde', kc, zeta, vc)

    def step(state, u_n):
        return chunk_decay[None, :, None, None] * state + u_n, state

    final, prev = lax.scan(step, init, u)
    return prev, final


def retention_scan(q, k, v, log_decay, init):
    B, T, H, d = q.shape
    i = jnp.arange(CHUNK, dtype=jnp.float32)
    diff = i[:, None] - i[None, :]
    intra = jnp.where(diff >= 0, jnp.exp(log_decay[:, None, None] * jnp.maximum(diff, 0.0)), 0.0)
    xi = jnp.exp(log_decay[:, None] * (i + 1.0))
    qc, kc, vc = _to_chunks(q), _to_chunks(k), _to_chunks(v)
    scores = jnp.einsum('bnihd,bnjhd->bnhij', qc, kc) * intra
    o_intra = jnp.einsum('bnhij,bnjhe->bnihe', scores, vc)
    prev, final = _chunk_states(kc, vc, log_decay, init)
    o_cross = jnp.einsum('bnihd,hi,nbhde->bnihe', qc, xi, prev)
    return (o_intra + o_cross).reshape(B, T, H, d), final


def fourier_mix(f):
    B, T, Cf = f.shape
    fg = f.astype(jnp.float32).reshape(B, T, FOURIER_GROUPS, Cf // FOURIER_GROUPS)
    out = jnp.fft.fft2(fg, axes=(1, 3), norm='ortho').real
    return out.reshape(B, T, Cf).astype(f.dtype)


def _heads(a, B, T):
    return a.astype(jnp.float32).reshape(B, T, RET_HEADS, RET_HEAD_DIM)


def token_mixers(p, log_decay_f, log_decay_b, init_f, init_b, conv_w, rope):
    B, T, _ = p.shape
    q = _heads(p[..., Q0:K0], B, T)
    k = _heads(p[..., K0:V0], B, T) * RET_HEAD_DIM ** -0.5
    v = _heads(p[..., V0:G0], B, T)
    if rope is not None:
        cos, sin = rope
        q = apply_rope(q, cos, sin)
        k = apply_rope(k, cos, sin)
    o_f, s_f = retention_scan(q, k, v, log_decay_f, init_f)
    o_b, s_b = retention_scan(q[:, ::-1], k[:, ::-1], v[:, ::-1], log_decay_b, init_b)
    o = o_f + o_b[:, ::-1]
    o = o * lax.rsqrt(jnp.mean(o * o, axis=-1, keepdims=True) + EPS)
    ret = o.reshape(B, T, RET_WIDTH).astype(p.dtype) * jax.nn.silu(p[..., G0:F0])
    four = fourier_mix(p[..., F0:CB0])
    conv = p[..., CB0:CC0] * dwconv3(p[..., CC0:CH0] * p[..., CH0:IN_WIDTH], conv_w)
    return jnp.concatenate([ret, four, conv], axis=-1), s_f, s_b


def conv_ffn(h, w_up, w_conv, w_down):
    u = dwconv3(h @ w_up, w_conv)
    val, gate = jnp.split(u, 2, axis=-1)
    return (val * jax.nn.silu(gate)) @ w_down


def setup_inputs(seed: int = 0) -> dict:
    key = jax.random.key(seed)
    ks = jax.random.split(key, 14)
    f32 = jnp.float32

    def nrm(k, shape, scale):
        return jax.random.normal(k, shape, f32) * scale

    base = np.log(2.0 ** (5.0 + np.arange(RET_HEADS)) - 1.0).astype(np.float32)
    return {
        'x': nrm(ks[0], (BATCH, SEQ, D_MODEL), 1.0),
        'c': nrm(ks[1], (BATCH, D_MODEL), 1.0),
        'ctx': nrm(ks[2], (BATCH, CTX_LEN, D_MODEL), 1.0),
        'c_ctx': nrm(ks[3], (D_MODEL,), 1.0),
        'w_mod': nrm(ks[4], (DEPTH, D_MODEL, 6 * D_MODEL), D_MODEL ** -0.5),
        'b_mod': nrm(ks[5], (DEPTH, 6 * D_MODEL), 0.02),
        'w_in': nrm(ks[6], (DEPTH, D_MODEL, IN_WIDTH), D_MODEL ** -0.5),
        'ret_decay_logit': jnp.asarray(base) + nrm(ks[7], (DEPTH, 2, RET_HEADS), 0.1),
        'mix_conv_w': nrm(ks[8], (DEPTH, 3, CONV_WIDTH), 3 ** -0.5),
        'w_out': nrm(ks[9], (DEPTH, MIX_WIDTH, D_MODEL), MIX_WIDTH ** -0.5),
        'ffn_w_up': nrm(ks[10], (DEPTH, D_MODEL, 2 * FFN_HIDDEN), D_MODEL ** -0.5),
        'ffn_conv_w': nrm(ks[11], (DEPTH, 3, 2 * FFN_HIDDEN), 3 ** -0.5),
        'ffn_w_down': nrm(ks[12], (DEPTH, FFN_HIDDEN, D_MODEL), FFN_HIDDEN ** -0.5),
        'final_norm_g': 1.0 + nrm(ks[13], (D_MODEL,), 0.05),
    }


def reference(x, c, ctx, c_ctx, w_mod, b_mod, w_in, ret_decay_logit, mix_conv_w, w_out,
              ffn_w_up, ffn_conv_w, ffn_w_down, final_norm_g):
    B, T, _ = x.shape
    Bc, Lc, _ = ctx.shape
    rope = axial_rope(T)
    zero_state = jnp.zeros((Bc, RET_HEADS, RET_HEAD_DIM, RET_HEAD_DIM), jnp.float32)
    silu_c = jax.nn.silu(c)
    silu_cc = jax.nn.silu(c_ctx)
    for l in range(DEPTH):
        last = l == DEPTH - 1
        sh1, sc1, g1, sh2, sc2, g2 = jnp.split((silu_c @ w_mod[l] + b_mod[l])[:, None, :], 6, axis=-1)
        csh1, csc1, cg1, csh2, csc2, cg2 = jnp.split(silu_cc @ w_mod[l] + b_mod[l], 6)
        log_decay = jax.nn.log_sigmoid(ret_decay_logit[l].astype(jnp.float32))

        hc = rmsnorm(ctx) * (1 + csc1) + csh1
        if last:
            kv = hc @ w_in[l][:, K0:G0]
            kc = _heads(kv[..., :RET_WIDTH], Bc, Lc) * RET_HEAD_DIM ** -0.5
            vc = _heads(kv[..., RET_WIDTH:], Bc, Lc)
            st_f = _chunk_states(_to_chunks(kc), _to_chunks(vc), log_decay[0], zero_state)[1]
            st_b = _chunk_states(_to_chunks(kc[:, ::-1]), _to_chunks(vc[:, ::-1]), log_decay[1], zero_state)[1]
        else:
            y_c, st_f, st_b = token_mixers(hc @ w_in[l], log_decay[0], log_decay[1],
                                           zero_state, zero_state, mix_conv_w[l], None)

        hx = rmsnorm(x) * (1 + sc1) + sh1
        y_x, _, _ = token_mixers(hx @ w_in[l], log_decay[0], log_decay[1],
                                 st_f, st_b, mix_conv_w[l], rope)
        x = x + g1 * (y_x @ w_out[l])
        x = x + g2 * conv_ffn(rmsnorm(x) * (1 + sc2) + sh2, ffn_w_up[l], ffn_conv_w[l], ffn_w_down[l])

        if not last:
            ctx = ctx + cg1 * (y_c @ w_out[l])
            ctx = ctx + cg2 * conv_ffn(rmsnorm(ctx) * (1 + csc2) + csh2, ffn_w_up[l], ffn_conv_w[l], ffn_w_down[l])
    return rmsnorm(x, final_norm_g)
```

```python
import functools

import jax
import jax.numpy as jnp
import numpy as np
from jax import lax
from jax.experimental import pallas as pl
from jax.experimental.pallas import tpu as pltpu

F32 = jnp.float32
BF16 = jnp.bfloat16

GRID_W = 64
RET_HEADS = 8
HEAD_DIM = 64
RET_WIDTH = RET_HEADS * HEAD_DIM
FOURIER_GROUPS = 4
FOURIER_GROUP_DIM = 64
FOURIER_WIDTH = FOURIER_GROUPS * FOURIER_GROUP_DIM
CONV_WIDTH = 256
ROPE_BASE = 10000.0
EPS = 1e-6
CHUNK = 128

LANES = 128
V7X_VMEM_LIMIT_BYTES = 60000 * 1024

HEAD_PAIRS = RET_WIDTH // LANES

Q0 = 0
K0 = Q0 + RET_WIDTH
V0 = K0 + RET_WIDTH
G0 = V0 + RET_WIDTH
F0 = G0 + RET_WIDTH
CB0 = F0 + FOURIER_WIDTH
IN_WIDTH = CB0 + 3 * CONV_WIDTH


def _params(n_grid_dims, vmem=V7X_VMEM_LIMIT_BYTES):
    return pltpu.CompilerParams(
        dimension_semantics=("arbitrary",) * n_grid_dims,
        vmem_limit_bytes=vmem,
    )


def _dot(a, b):
    return jnp.dot(a, b, preferred_element_type=F32)


def _mod_norm(x, shift, scale):
    ms = jnp.mean(x * x, axis=-1, keepdims=True)
    return (x * lax.rsqrt(ms + EPS)) * (1.0 + scale) + shift


def _mod_kernel(c_ref, w_ref, b_ref, o_ref):
    c = c_ref[...]
    s = c * jax.nn.sigmoid(c)
    o_ref[0] = jnp.dot(s, w_ref[0], preferred_element_type=F32,
                       precision=lax.Precision.HIGHEST) + b_ref[0]


def _modulation(c_all, w_mod, b_mod):
    depth, d, n = w_mod.shape
    rows = c_all.shape[0]
    tn = 1024
    return pl.pallas_call(
        _mod_kernel,
        out_shape=jax.ShapeDtypeStruct((depth, rows, n), F32),
        grid=(depth, n // tn),
        in_specs=[
            pl.BlockSpec((rows, d), lambda l, j: (0, 0)),
            pl.BlockSpec((1, d, tn), lambda l, j: (l, 0, j)),
            pl.BlockSpec((1, 1, tn), lambda l, j: (l, 0, j)),
        ],
        out_specs=pl.BlockSpec((1, rows, tn), lambda l, j: (l, 0, j)),
        compiler_params=_params(2),
        name="modulation",
    )(c_all, w_mod, b_mod.reshape(depth, 1, n))


def _rope_block(t, cos, sin, first_half):
    lo = pltpu.roll(t, LANES - HEAD_DIM // 2, 1)
    hi = pltpu.roll(t, HEAD_DIM // 2, 1)
    return t * cos + jnp.where(first_half, lo, hi) * sin


def _in_proj_kernel(*refs, d_model, use_rope, kv_only):
    it = iter(refs)
    x_ref, mod_ref, w_ref = next(it), next(it), next(it)
    cos_ref = sin_ref = dft_ref = None
    if use_rope:
        cos_ref, sin_ref = next(it), next(it)
    if not kv_only:
        dft_ref = next(it)
    outs = list(it)

    nb, tt, d = x_ref.shape
    m = nb * tt
    x = x_ref[...].reshape(m, d)
    shift = mod_ref[0, :, 0:d_model]
    scale = mod_ref[0, :, d_model:2 * d_model]
    h = _mod_norm(x, shift, scale).astype(BF16)

    def put(ref, col, val):
        ref[:, :, col:col + val.shape[1]] = val.astype(ref.dtype).reshape(nb, tt, val.shape[1])

    if use_rope:
        cos = cos_ref[...]
        sin = sin_ref[...]
        lane = lax.broadcasted_iota(jnp.int32, (m, LANES), 1)
        first_half = (lane % HEAD_DIM) < (HEAD_DIM // 2)

    def rotary(t, ref, col, mult):
        for j in range(HEAD_PAIRS):
            tb = t[:, j * LANES:(j + 1) * LANES]
            if use_rope:
                tb = _rope_block(tb, cos, sin, first_half)
            put(ref, col + j * LANES, tb * mult if mult != 1.0 else tb)

    k_scale = HEAD_DIM ** -0.5
    if kv_only:
        (kv_ref,) = outs
        rotary(_dot(h, w_ref[:, K0:V0]), kv_ref, 0, k_scale)
        put(kv_ref, RET_WIDTH, _dot(h, w_ref[:, V0:G0]))
        return

    qkvg_ref, z_ref, pc_ref = outs
    rotary(_dot(h, w_ref[:, Q0:K0]), qkvg_ref, 0, 1.0)
    rotary(_dot(h, w_ref[:, K0:V0]), qkvg_ref, RET_WIDTH, k_scale)
    put(qkvg_ref, 2 * RET_WIDTH, _dot(h, w_ref[:, V0:G0]))
    put(qkvg_ref, 3 * RET_WIDTH, _dot(h, w_ref[:, G0:F0]))
    f = _dot(h, w_ref[:, F0:CB0]).astype(BF16)
    put(z_ref, 0, _dot(f, dft_ref[...]))
    put(pc_ref, 0, _dot(h, w_ref[:, CB0:IN_WIDTH]))


def _in_proj(x, mod, w_in, rope, dft_c, *, nb, tt, kv_only=False):
    b, t, d = x.shape
    use_rope = rope is not None
    shared_mod = mod.shape[0] == 1
    assert nb == 1 or (shared_mod and not use_rope)
    mod_map = (lambda i, j: (0, 0, 0)) if shared_mod else (lambda i, j: (i, 0, 0))
    in_specs = [
        pl.BlockSpec((nb, tt, d), lambda i, j: (i, j, 0)),
        pl.BlockSpec((1, 1, mod.shape[2]), mod_map),
        pl.BlockSpec(w_in.shape, lambda i, j: (0, 0)),
    ]
    args = [x, mod, w_in]
    if use_rope:
        in_specs += [pl.BlockSpec((tt, LANES), lambda i, j: (j, 0))] * 2
        args += list(rope)
    if kv_only:
        widths = (2 * RET_WIDTH,)
    else:
        in_specs.append(pl.BlockSpec(dft_c.shape, lambda i, j: (0, 0)))
        args.append(dft_c)
        widths = (4 * RET_WIDTH, 2 * FOURIER_WIDTH, 3 * CONV_WIDTH)
    out_shape = tuple(jax.ShapeDtypeStruct((b, t, w), BF16) for w in widths)
    out_specs = tuple(pl.BlockSpec((nb, tt, w), lambda i, j: (i, j, 0)) for w in widths)
    return pl.pallas_call(
        functools.partial(_in_proj_kernel, d_model=d, use_rope=use_rope, kv_only=kv_only),
        out_shape=out_shape,
        grid=(b // nb, t // tt),
        in_specs=in_specs,
        out_specs=out_specs,
        compiler_params=_params(2),
        name="in_proj_kv" if kv_only else "in_proj",
    )(*args)


def _pair_masks(rows):
    lane = lax.broadcasted_iota(jnp.int32, (rows, LANES), 1)
    return lane < HEAD_DIM


def _state_update(state_ref, p, k, v, zeta, decay, diag):
    kz = (k.astype(F32) * zeta).astype(BF16)
    u = lax.dot_general(kz, v, (((0,), (0,)), ((), ())), preferred_element_type=F32)
    state_ref[p] = decay * state_ref[p] + jnp.where(diag, u, 0.0)


def _retention_states_kernel(kv_ref, zf_ref, zb_ref, cdf_ref, cdb_ref, sf_out, sb_out, *, t):
    n_chunks = t // CHUNK
    sf_out[...] = jnp.zeros(sf_out.shape, F32)
    sb_out[...] = jnp.zeros(sb_out.shape, F32)
    row = lax.broadcasted_iota(jnp.int32, (LANES, LANES), 0)
    col = lax.broadcasted_iota(jnp.int32, (LANES, LANES), 1)
    diag = (row < HEAD_DIM) == (col < HEAD_DIM)

    def body(n, carry):
        rf = pl.multiple_of(n * CHUNK, CHUNK)
        rb = pl.multiple_of((n_chunks - 1 - n) * CHUNK, CHUNK)
        for p in range(HEAD_PAIRS):
            lanes = slice(p * LANES, (p + 1) * LANES)
            vl = slice(RET_WIDTH + p * LANES, RET_WIDTH + (p + 1) * LANES)
            _state_update(sf_out.at[0], p, kv_ref[0, pl.ds(rf, CHUNK), lanes],
                          kv_ref[0, pl.ds(rf, CHUNK), vl], zf_ref[:, lanes], cdf_ref[:, lanes], diag)
            _state_update(sb_out.at[0], p, kv_ref[0, pl.ds(rb, CHUNK), lanes],
                          kv_ref[0, pl.ds(rb, CHUNK), vl], zb_ref[:, lanes], cdb_ref[:, lanes], diag)
        return carry

    lax.fori_loop(0, n_chunks, body, 0)


def _retention_states(kv, tables):
    b, t, _ = kv.shape
    state = jax.ShapeDtypeStruct((b, HEAD_PAIRS, LANES, LANES), F32)
    tab = lambda a: pl.BlockSpec(a.shape, lambda i: (0,) * a.ndim)
    ins = [tables["zeta_f"], tables["zeta_b"], tables["cd_f"], tables["cd_b"]]
    return pl.pallas_call(
        functools.partial(_retention_states_kernel, t=t),
        out_shape=(state, state),
        grid=(b,),
        in_specs=[pl.BlockSpec((1, t, kv.shape[2]), lambda i: (i, 0, 0))] + [tab(a) for a in ins],
        out_specs=(pl.BlockSpec((1, HEAD_PAIRS, LANES, LANES), lambda i: (i, 0, 0, 0)),) * 2,
        compiler_params=_params(1),
        name="retention_states",
    )(kv, *ins)


def _mixers_kernel(*refs, t, has_init, out_states):
    it = iter(refs)
    q_ref, k_ref, v_ref, g_ref, pc_ref = (next(it) for _ in range(5))
    dmask_ref, xif_ref, xib_ref, zf_ref, zb_ref, cdf_ref, cdb_ref, cw_ref = (next(it) for _ in range(8))
    if has_init:
        if_ref, ib_ref = next(it), next(it)
    ret_ref, conv_ref = next(it), next(it)
    if out_states:
        sf_out, sb_out = next(it), next(it)
    sf_ref, sb_ref, sb_all, cm_ref = next(it), next(it), next(it), next(it)

    n_chunks = t // CHUNK
    if has_init:
        sf_ref[...] = if_ref[0]
        sb_ref[...] = ib_ref[0]
    else:
        sf_ref[...] = jnp.zeros(sf_ref.shape, F32)
        sb_ref[...] = jnp.zeros(sb_ref.shape, F32)

    row = lax.broadcasted_iota(jnp.int32, (LANES, LANES), 0)
    col = lax.broadcasted_iota(jnp.int32, (LANES, LANES), 1)
    diag = (row < HEAD_DIM) == (col < HEAD_DIM)
    head0 = _pair_masks(CHUNK)

    def bwd(i, carry):
        n = n_chunks - 1 - i
        r0 = pl.multiple_of(n * CHUNK, CHUNK)
        for p in range(HEAD_PAIRS):
            lanes = slice(p * LANES, (p + 1) * LANES)
            sb_all[n, p] = sb_ref[p]
            _state_update(sb_ref, p, k_ref[0, pl.ds(r0, CHUNK), lanes], v_ref[0, pl.ds(r0, CHUNK), lanes],
                          zb_ref[:, lanes], cdb_ref[:, lanes], diag)
        return carry

    lax.fori_loop(0, n_chunks, bwd, 0)

    def fwd(n, carry):
        r0 = pl.multiple_of(n * CHUNK, CHUNK)
        rows = pl.ds(r0, CHUNK)
        for p in range(HEAD_PAIRS):
            lanes = slice(p * LANES, (p + 1) * LANES)
            q = q_ref[0, rows, lanes]
            k = k_ref[0, rows, lanes]
            v = v_ref[0, rows, lanes]
            zero = jnp.zeros_like(k)
            o = None
            for hh in range(2):
                keep = head0 if hh == 0 else jnp.logical_not(head0)
                kh = jnp.where(keep, k, zero)
                vh = jnp.where(keep, v, zero)
                s = lax.dot_general(q, kh, (((1,), (1,)), ((), ())), preferred_element_type=F32)
                pr = (s * dmask_ref[2 * p + hh]).astype(BF16)
                oh = _dot(pr, vh)
                o = oh if o is None else o + oh
            qf = q.astype(F32)
            o = o + _dot((qf * xif_ref[:, lanes]).astype(BF16), sf_ref[p].astype(BF16))
            o = o + _dot((qf * xib_ref[:, lanes]).astype(BF16), sb_all[n, p].astype(BF16))
            o2 = o * o
            s0 = jnp.sum(jnp.where(head0, o2, 0.0), axis=-1, keepdims=True)
            s1 = jnp.sum(jnp.where(head0, 0.0, o2), axis=-1, keepdims=True)
            r0n = lax.rsqrt(s0 * (1.0 / HEAD_DIM) + EPS)
            r1n = lax.rsqrt(s1 * (1.0 / HEAD_DIM) + EPS)
            o = o * jnp.where(head0, r0n, r1n)
            g = g_ref[0, rows, lanes].astype(F32)
            ret_ref[0, rows, lanes] = (o * (g * jax.nn.sigmoid(g))).astype(ret_ref.dtype)
            _state_update(sf_ref, p, k, v, zf_ref[:, lanes], cdf_ref[:, lanes], diag)
        return carry

    lax.fori_loop(0, n_chunks, fwd, 0)

    if out_states:
        sf_out[0] = sf_ref[...]
        sb_out[0] = sb_ref[...]

    pad = 8
    cm_ref[0:pad, :] = jnp.zeros((pad, CONV_WIDTH), F32)
    cm_ref[pad + t:pad + t + pad, :] = jnp.zeros((pad, CONV_WIDTH), F32)
    rb = min(t, 256)
    for c in range(t // rb):
        rs = slice(c * rb, (c + 1) * rb)
        cm_ref[pad + c * rb:pad + (c + 1) * rb, :] = (
            pc_ref[0, rs, CONV_WIDTH:2 * CONV_WIDTH].astype(F32)
            * pc_ref[0, rs, 2 * CONV_WIDTH:3 * CONV_WIDTH].astype(F32))
    w = cw_ref[...]
    for c in range(t // rb):
        base = pad + c * rb
        acc = (cm_ref[base - 1:base - 1 + rb, :] * w[0:1, :]
               + cm_ref[base:base + rb, :] * w[1:2, :]
               + cm_ref[base + 1:base + 1 + rb, :] * w[2:3, :])
        cb = pc_ref[0, c * rb:(c + 1) * rb, 0:CONV_WIDTH].astype(F32)
        conv_ref[0, c * rb:(c + 1) * rb, :] = (cb * acc).astype(conv_ref.dtype)


def _mixers(qkvg, pc, tables, conv_w, init, *, out_states):
    b, t, _ = qkvg.shape
    has_init = init is not None
    n_chunks = t // CHUNK
    col = lambda j: pl.BlockSpec((1, t, RET_WIDTH), lambda i, j=j: (i, 0, j))
    tab = lambda a: pl.BlockSpec(a.shape, lambda i: (0,) * a.ndim)
    tabs = [tables[k] for k in ("dmask", "xi_f", "xi_b", "zeta_f", "zeta_b", "cd_f", "cd_b")] + [conv_w]
    in_specs = [col(0), col(1), col(2), col(3),
                pl.BlockSpec((1, t, pc.shape[2]), lambda i: (i, 0, 0))] + [tab(a) for a in tabs]
    args = [qkvg, qkvg, qkvg, qkvg, pc] + tabs
    state_spec = pl.BlockSpec((1, HEAD_PAIRS, LANES, LANES), lambda i: (i, 0, 0, 0))
    if has_init:
        in_specs += [state_spec, state_spec]
        args += list(init)
    out_shape = [jax.ShapeDtypeStruct((b, t, RET_WIDTH), BF16),
                 jax.ShapeDtypeStruct((b, t, CONV_WIDTH), BF16)]
    out_specs = [pl.BlockSpec((1, t, RET_WIDTH), lambda i: (i, 0, 0)),
                 pl.BlockSpec((1, t, CONV_WIDTH), lambda i: (i, 0, 0))]
    if out_states:
        out_shape += [jax.ShapeDtypeStruct((b, HEAD_PAIRS, LANES, LANES), F32)] * 2
        out_specs += [state_spec, state_spec]
    return pl.pallas_call(
        functools.partial(_mixers_kernel, t=t, has_init=has_init, out_states=out_states),
        out_shape=tuple(out_shape),
        grid=(b,),
        in_specs=in_specs,
        out_specs=tuple(out_specs),
        scratch_shapes=[
            pltpu.VMEM((HEAD_PAIRS, LANES, LANES), F32),
            pltpu.VMEM((HEAD_PAIRS, LANES, LANES), F32),
            pltpu.VMEM((n_chunks, HEAD_PAIRS, LANES, LANES), F32),
            pltpu.VMEM((t + 16, CONV_WIDTH), F32),
        ],
        compiler_params=_params(1),
        name="mixers",
    )(*args)


def _fourier_kernel(cm_ref, sm_ref, z_ref, o_ref, *, scale):
    zr = z_ref[0, :, 0:FOURIER_WIDTH]
    zi = z_ref[0, :, FOURIER_WIDTH:2 * FOURIER_WIDTH]
    o = _dot(cm_ref[...], zr) - _dot(sm_ref[...], zi)
    o_ref[0] = (o * scale).astype(o_ref.dtype)


def _fourier(z, cos_m, sin_m):
    b, t, _ = z.shape
    tb = min(t, 1024)
    scale = float(1.0 / np.sqrt(float(t) * FOURIER_GROUP_DIM))
    return pl.pallas_call(
        functools.partial(_fourier_kernel, scale=scale),
        out_shape=jax.ShapeDtypeStruct((b, t, FOURIER_WIDTH), BF16),
        grid=(t // tb, b),
        in_specs=[
            pl.BlockSpec((tb, t), lambda i, j: (i, 0)),
            pl.BlockSpec((tb, t), lambda i, j: (i, 0)),
            pl.BlockSpec((1, t, z.shape[2]), lambda i, j: (j, 0, 0)),
        ],
        out_specs=pl.BlockSpec((1, tb, FOURIER_WIDTH), lambda i, j: (j, i, 0)),
        compiler_params=_params(2),
        name="fourier",
    )(cos_m, sin_m, z)


def _out_proj_kernel(x_ref, mod_ref, ret_ref, four_ref, conv_ref, w_ref, o_ref, *, d_model):
    nb, tt, d = x_ref.shape
    m = nb * tt
    acc = _dot(ret_ref[...].reshape(m, RET_WIDTH), w_ref[0:RET_WIDTH, :])
    acc = acc + _dot(four_ref[...].reshape(m, FOURIER_WIDTH), w_ref[RET_WIDTH:RET_WIDTH + FOURIER_WIDTH, :])
    acc = acc + _dot(conv_ref[...].reshape(m, CONV_WIDTH), w_ref[RET_WIDTH + FOURIER_WIDTH:, :])
    gate = mod_ref[0, :, 2 * d_model:3 * d_model]
    o_ref[...] = (x_ref[...].reshape(m, d) + gate * acc).reshape(nb, tt, d)


def _out_proj(x, mod, ret, four, conv, w_out, *, nb, tt):
    b, t, d = x.shape
    shared_mod = mod.shape[0] == 1
    assert nb == 1 or shared_mod
    mod_map = (lambda i, j: (0, 0, 0)) if shared_mod else (lambda i, j: (i, 0, 0))
    blk = lambda w: pl.BlockSpec((nb, tt, w), lambda i, j: (i, j, 0))
    return pl.pallas_call(
        functools.partial(_out_proj_kernel, d_model=d),
        out_shape=jax.ShapeDtypeStruct(x.shape, F32),
        grid=(b // nb, t // tt),
        in_specs=[blk(d), pl.BlockSpec((1, 1, mod.shape[2]), mod_map),
                  blk(RET_WIDTH), blk(FOURIER_WIDTH), blk(CONV_WIDTH),
                  pl.BlockSpec(w_out.shape, lambda i, j: (0, 0))],
        out_specs=blk(d),
        compiler_params=_params(2),
        name="out_proj",
    )(x, mod, ret, four, conv, w_out)


def _dwconv3_rows(u, w, no_prev, no_next):
    m = u.shape[0]
    prev = jnp.where(no_prev, 0.0, pltpu.roll(u, 1, 0))
    nxt = jnp.where(no_next, 0.0, pltpu.roll(u, m - 1, 0))
    return prev * w[0:1, :] + u * w[1:2, :] + nxt * w[2:3, :]


def _ffn_kernel(*refs, d_model, seq, final_norm):
    it = iter(refs)
    x_ref, mod_ref, wv_ref, wg_ref, cv_ref, cg_ref, wd_ref = (next(it) for _ in range(7))
    fg_ref = next(it) if final_norm else None
    o_ref, h_ref = next(it), next(it)

    nb, tt, d = x_ref.shape
    m = nb * tt
    j = pl.program_id(1)

    @pl.when(j == 0)
    def _():
        shift = mod_ref[0, :, 3 * d_model:4 * d_model]
        scale = mod_ref[0, :, 4 * d_model:5 * d_model]
        h_ref[...] = _mod_norm(x_ref[...].reshape(m, d), shift, scale).astype(BF16)
        o_ref[...] = jnp.zeros(o_ref.shape, F32)

    h = h_ref[...]
    pos = lax.broadcasted_iota(jnp.int32, (m, 1), 0) % seq
    no_prev = pos == 0
    no_next = pos == seq - 1
    val = _dwconv3_rows(_dot(h, wv_ref[...]), cv_ref[...], no_prev, no_next)
    gate = _dwconv3_rows(_dot(h, wg_ref[...]), cg_ref[...], no_prev, no_next)
    act = (val * (gate * jax.nn.sigmoid(gate))).astype(BF16)
    o_ref[...] += _dot(act, wd_ref[...]).reshape(nb, tt, d)

    @pl.when(j == pl.num_programs(1) - 1)
    def _():
        gate2 = mod_ref[0, :, 5 * d_model:6 * d_model]
        y = x_ref[...].reshape(m, d) + gate2 * o_ref[...].reshape(m, d)
        if final_norm:
            ms = jnp.mean(y * y, axis=-1, keepdims=True)
            y = (y * lax.rsqrt(ms + EPS)) * fg_ref[...]
        o_ref[...] = y.reshape(nb, tt, d)


def _ffn(x, mod, w_up, conv_w, w_down, final_g, *, nb, fh):
    b, t, d = x.shape
    f = w_down.shape[0]
    nj = f // fh
    assert nj * fh == f
    shared_mod = mod.shape[0] == 1
    assert nb == 1 or shared_mod
    mod_map = (lambda i, j: (0, 0, 0)) if shared_mod else (lambda i, j: (i, 0, 0))
    final_norm = final_g is not None
    in_specs = [
        pl.BlockSpec((nb, t, d), lambda i, j: (i, 0, 0)),
        pl.BlockSpec((1, 1, mod.shape[2]), mod_map),
        pl.BlockSpec((d, fh), lambda i, j: (0, j)),
        pl.BlockSpec((d, fh), lambda i, j: (0, nj + j)),
        pl.BlockSpec((3, fh), lambda i, j: (0, j)),
        pl.BlockSpec((3, fh), lambda i, j: (0, nj + j)),
        pl.BlockSpec((fh, d), lambda i, j: (j, 0)),
    ]
    args = [x, mod, w_up, w_up, conv_w, conv_w, w_down]
    if final_norm:
        in_specs.append(pl.BlockSpec((1, d), lambda i, j: (0, 0)))
        args.append(final_g.reshape(1, d))
    return pl.pallas_call(
        functools.partial(_ffn_kernel, d_model=d, seq=t, final_norm=final_norm),
        out_shape=jax.ShapeDtypeStruct(x.shape, F32),
        grid=(b // nb, nj),
        in_specs=in_specs,
        out_specs=pl.BlockSpec((nb, t, d), lambda i, j: (i, 0, 0)),
        scratch_shapes=[pltpu.VMEM((nb * t, d), BF16)],
        compiler_params=_params(2),
        name="ffn",
    )(*args)


def _rope_tables(t):
    rows = t // GRID_W
    row = jnp.repeat(jnp.arange(rows, dtype=F32), GRID_W)
    col = jnp.tile(jnp.arange(GRID_W, dtype=F32), rows)
    n_freq = HEAD_DIM // 4
    freq = ROPE_BASE ** (-jnp.arange(n_freq, dtype=F32) / n_freq)
    ang = jnp.concatenate([row[:, None] * freq, col[:, None] * freq], axis=-1)
    reps = LANES // ang.shape[1]
    cos = jnp.tile(jnp.cos(ang), (1, reps))
    sign = jnp.where((jnp.arange(LANES) % HEAD_DIM) < HEAD_DIM // 2, -1.0, 1.0).astype(F32)
    sin = jnp.tile(jnp.sin(ang), (1, reps)) * sign
    return cos, sin


def _decay_tables(decay_logit):
    ld = jax.nn.log_sigmoid(decay_logit.astype(F32))
    ld_f, ld_b = ld[0], ld[1]
    i = jnp.arange(CHUNK, dtype=F32)
    lane_f = jnp.repeat(ld_f, HEAD_DIM)[None, :]
    lane_b = jnp.repeat(ld_b, HEAD_DIM)[None, :]
    diff = i[:, None] - i[None, :]
    dmask = jnp.where(diff > 0, jnp.exp(ld_f[:, None, None] * jnp.maximum(diff, 0.0)),
                      jnp.where(diff < 0, jnp.exp(ld_b[:, None, None] * jnp.maximum(-diff, 0.0)), 2.0))
    return {
        "dmask": dmask,
        "xi_f": jnp.exp(lane_f * (i[:, None] + 1.0)),
        "xi_b": jnp.exp(lane_b * (CHUNK - i[:, None])),
        "zeta_f": jnp.exp(lane_f * (CHUNK - 1.0 - i[:, None])),
        "zeta_b": jnp.exp(lane_b * i[:, None]),
        "cd_f": jnp.exp(lane_f * CHUNK),
        "cd_b": jnp.exp(lane_b * CHUNK),
    }


def _position_dft(t):
    idx = jnp.arange(t, dtype=jnp.int32)
    ang = ((idx[:, None] * idx[None, :]) % t).astype(F32) * (2.0 * np.pi / t)
    return jnp.cos(ang).astype(BF16), jnp.sin(ang).astype(BF16)


def _channel_dft():
    n = FOURIER_GROUP_DIM
    idx = jnp.arange(n, dtype=jnp.int32)
    ang = ((idx[:, None] * idx[None, :]) % n).astype(F32) * (2.0 * np.pi / n)
    eye = jnp.eye(FOURIER_GROUPS, dtype=F32)
    return jnp.concatenate([jnp.kron(eye, jnp.cos(ang)), jnp.kron(eye, jnp.sin(ang))], axis=1).astype(BF16)


def kernel(x, c, ctx, c_ctx, w_mod, b_mod, w_in, ret_decay_logit, mix_conv_w, w_out,
           ffn_w_up, ffn_conv_w, ffn_w_down, final_norm_g):
    b, t, d = x.shape
    _, lc, _ = ctx.shape
    depth = w_mod.shape[0]

    rows = ((b + 1 + 7) // 8) * 8
    c_all = jnp.zeros((rows, d), F32).at[:b].set(c).at[b].set(c_ctx)
    mod_all = _modulation(c_all, w_mod, b_mod)

    rope = _rope_tables(t)
    dft_c = _channel_dft()
    dft_x = _position_dft(t)
    dft_ctx = _position_dft(lc)

    tt = min(t, 512)
    nb_ctx = max(1, min(b, 1024 // lc))
    fh = 256

    for l in range(depth):
        last = l == depth - 1
        mod_x = mod_all[l, :b].reshape(b, 1, 6 * d)
        mod_c = mod_all[l, b:b + 1].reshape(1, 1, 6 * d)
        w_in_l = w_in[l].astype(BF16)
        w_out_l = w_out[l].astype(BF16)
        w_up_l = ffn_w_up[l].astype(BF16)
        w_down_l = ffn_w_down[l].astype(BF16)
        tables = _decay_tables(ret_decay_logit[l])

        if last:
            (kv,) = _in_proj(ctx, mod_c, w_in_l, None, dft_c, nb=nb_ctx, tt=lc, kv_only=True)
            st_f, st_b = _retention_states(kv, tables)
        else:
            qkvg_c, z_c, pc_c = _in_proj(ctx, mod_c, w_in_l, None, dft_c, nb=nb_ctx, tt=lc)
            ret_c, conv_c, st_f, st_b = _mixers(qkvg_c, pc_c, tables, mix_conv_w[l], None, out_states=True)
            four_c = _fourier(z_c, *dft_ctx)

        qkvg, z, pc = _in_proj(x, mod_x, w_in_l, rope, dft_c, nb=1, tt=tt)
        ret, conv = _mixers(qkvg, pc, tables, mix_conv_w[l], (st_f, st_b), out_states=False)
        four = _fourier(z, *dft_x)
        x = _out_proj(x, mod_x, ret, four, conv, w_out_l, nb=1, tt=tt)
        x = _ffn(x, mod_x, w_up_l, ffn_conv_w[l], w_down_l, final_norm_g if last else None, nb=1, fh=fh)

        if not last:
            ctx = _out_proj(ctx, mod_c, ret_c, four_c, conv_c, w_out_l, nb=nb_ctx, tt=lc)
            ctx = _ffn(ctx, mod_c, w_up_l, ffn_conv_w[l], w_down_l, None, nb=nb_ctx, fh=fh)
    return x
```

```python
import functools

import jax
import jax.numpy as jnp
import numpy as np
from jax import lax
from jax.experimental import pallas as pl
from jax.experimental.pallas import tpu as pltpu

F32 = jnp.float32
BF16 = jnp.bfloat16

GRID_W = 64
RET_HEADS = 8
HEAD_DIM = 64
RET_WIDTH = RET_HEADS * HEAD_DIM
FOURIER_GROUPS = 4
FOURIER_GROUP_DIM = 64
FOURIER_WIDTH = FOURIER_GROUPS * FOURIER_GROUP_DIM
CONV_WIDTH = 256
ROPE_BASE = 10000.0
EPS = 1e-6
CHUNK = 128

LANES = 128
BF16_SUBLANES = 16
V7X_VMEM_LIMIT_BYTES = 60000 * 1024

HEAD_PAIRS = RET_WIDTH // LANES
HALO = BF16_SUBLANES

Q0 = 0
K0 = Q0 + RET_WIDTH
V0 = K0 + RET_WIDTH
G0 = V0 + RET_WIDTH
F0 = G0 + RET_WIDTH
CB0 = F0 + FOURIER_WIDTH
IN_WIDTH = CB0 + 3 * CONV_WIDTH


def _params(n_grid_dims, vmem=V7X_VMEM_LIMIT_BYTES):
    return pltpu.CompilerParams(
        dimension_semantics=("arbitrary",) * n_grid_dims,
        vmem_limit_bytes=vmem,
    )


def _dot(a, b):
    return jnp.dot(a, b, preferred_element_type=F32)


def _mod_norm(x, shift, scale):
    ms = jnp.mean(x * x, axis=-1, keepdims=True)
    return (x * lax.rsqrt(ms + EPS)) * (1.0 + scale) + shift


def _resident(shape):
    return pl.BlockSpec(shape, lambda *_: (0,) * len(shape), pipeline_mode=pl.Buffered(1))


def _mod_kernel(c_ref, w_ref, b_ref, o_ref):
    c = c_ref[...]
    s = c * jax.nn.sigmoid(c)
    o_ref[0] = jnp.dot(s, w_ref[0], preferred_element_type=F32,
                       precision=lax.Precision.HIGHEST) + b_ref[0]


def _modulation(c_all, w_mod, b_mod):
    depth, d, n = w_mod.shape
    rows = c_all.shape[0]
    tn = 1024
    return pl.pallas_call(
        _mod_kernel,
        out_shape=jax.ShapeDtypeStruct((depth, rows, n), F32),
        grid=(depth, n // tn),
        in_specs=[
            pl.BlockSpec((rows, d), lambda l, j: (0, 0)),
            pl.BlockSpec((1, d, tn), lambda l, j: (l, 0, j)),
            pl.BlockSpec((1, 1, tn), lambda l, j: (l, 0, j)),
        ],
        out_specs=pl.BlockSpec((1, rows, tn), lambda l, j: (l, 0, j)),
        compiler_params=_params(2),
        name="modulation",
    )(c_all, w_mod, b_mod.reshape(depth, 1, n))


def _rope_block(t, cos, sin, first_half):
    lo = pltpu.roll(t, LANES - HEAD_DIM // 2, 1)
    hi = pltpu.roll(t, HEAD_DIM // 2, 1)
    return t * cos + jnp.where(first_half, lo, hi) * sin


def _in_proj_kernel(*refs, d_model, use_rope, kv_only):
    it = iter(refs)
    x_ref, mod_ref, w_ref, zf_ref, zb_ref = (next(it) for _ in range(5))
    cos_ref = sin_ref = dft_ref = None
    if use_rope:
        cos_ref, sin_ref = next(it), next(it)
    if not kv_only:
        dft_ref = next(it)
    outs = list(it)

    nb, tt, d = x_ref.shape
    m = nb * tt
    x = x_ref[...].reshape(m, d)
    shift = mod_ref[0, :, 0:d_model]
    scale = mod_ref[0, :, d_model:2 * d_model]
    h = _mod_norm(x, shift, scale).astype(BF16)

    def put(ref, col, val):
        ref[:, :, col:col + val.shape[1]] = val.astype(ref.dtype).reshape(nb, tt, val.shape[1])

    if use_rope:
        cos = cos_ref[...]
        sin = sin_ref[...]
        lane = lax.broadcasted_iota(jnp.int32, (m, LANES), 1)
        first_half = (lane % HEAD_DIM) < (HEAD_DIM // 2)

    def rotary(t, j):
        tb = t[:, j * LANES:(j + 1) * LANES]
        return _rope_block(tb, cos, sin, first_half) if use_rope else tb

    def chunk_tiled(ref, j):
        tab = ref[:, j * LANES:(j + 1) * LANES]
        return jnp.concatenate([tab] * (m // CHUNK), axis=0)

    kv_ref = outs[0] if kv_only else outs[1]
    k = _dot(h, w_ref[:, K0:V0])
    for j in range(HEAD_PAIRS):
        kb = rotary(k, j) * (HEAD_DIM ** -0.5)
        if not kv_only:
            put(outs[0], RET_WIDTH + j * LANES, kb)
        put(kv_ref, j * LANES, kb * chunk_tiled(zf_ref, j))
        put(kv_ref, RET_WIDTH + j * LANES, kb * chunk_tiled(zb_ref, j))
    v = _dot(h, w_ref[:, V0:G0])
    if kv_only:
        put(kv_ref, 2 * RET_WIDTH, v)
        return

    qkvg_ref, _, z_ref, pc_ref = outs
    put(qkvg_ref, 2 * RET_WIDTH, v)
    q = _dot(h, w_ref[:, Q0:K0])
    for j in range(HEAD_PAIRS):
        put(qkvg_ref, j * LANES, rotary(q, j))
    put(qkvg_ref, 3 * RET_WIDTH, _dot(h, w_ref[:, G0:F0]))
    f = _dot(h, w_ref[:, F0:CB0]).astype(BF16)
    put(z_ref, 0, _dot(f, dft_ref[...]))
    put(pc_ref, 0, _dot(h, w_ref[:, CB0:IN_WIDTH]))


def _in_proj(x, mod, w_in, tables, rope, dft_c, *, nb, tt, kv_only=False):
    b, t, d = x.shape
    use_rope = rope is not None
    shared_mod = mod.shape[0] == 1
    assert nb == 1 or (shared_mod and not use_rope)
    assert tt % CHUNK == 0
    mod_map = (lambda i, j: (0, 0, 0)) if shared_mod else (lambda i, j: (i, 0, 0))
    in_specs = [
        pl.BlockSpec((nb, tt, d), lambda i, j: (i, j, 0)),
        pl.BlockSpec((1, 1, mod.shape[2]), mod_map),
        _resident(w_in.shape),
        _resident(tables["zeta_f"].shape),
        _resident(tables["zeta_b"].shape),
    ]
    args = [x, mod, w_in, tables["zeta_f"], tables["zeta_b"]]
    if use_rope:
        in_specs += [pl.BlockSpec((tt, LANES), lambda i, j: (j, 0))] * 2
        args += list(rope)
    if kv_only:
        widths = (3 * RET_WIDTH,)
    else:
        in_specs.append(_resident(dft_c.shape))
        args.append(dft_c)
        widths = (4 * RET_WIDTH, 2 * RET_WIDTH, 2 * FOURIER_WIDTH, 3 * CONV_WIDTH)
    out_shape = tuple(jax.ShapeDtypeStruct((b, t, w), BF16) for w in widths)
    out_specs = tuple(pl.BlockSpec((nb, tt, w), lambda i, j: (i, j, 0)) for w in widths)
    return pl.pallas_call(
        functools.partial(_in_proj_kernel, d_model=d, use_rope=use_rope, kv_only=kv_only),
        out_shape=out_shape,
        grid=(b // nb, t // tt),
        in_specs=in_specs,
        out_specs=out_specs,
        compiler_params=_params(2),
        name="in_proj_kv" if kv_only else "in_proj",
    )(*args)


def _block_diag_mask():
    row = lax.broadcasted_iota(jnp.int32, (LANES, LANES), 0)
    col = lax.broadcasted_iota(jnp.int32, (LANES, LANES), 1)
    return (row < HEAD_DIM) == (col < HEAD_DIM)


def _state_sweep(kz_ref, v_ref, v_col, cdf_ref, cdb_ref, sf_ref, sb_ref, s_all, n_chunks):
    diag = _block_diag_mask()

    def update(state_ref, p, kz, v, decay):
        u = lax.dot_general(kz, v, (((0,), (0,)), ((), ())), preferred_element_type=F32)
        state_ref[p] = decay * state_ref[p] + jnp.where(diag, u, 0.0)

    def body(i, carry):
        rf = pl.multiple_of(i * CHUNK, CHUNK)
        nb_ = n_chunks - 1 - i
        rb = pl.multiple_of(nb_ * CHUNK, CHUNK)
        for p in range(HEAD_PAIRS):
            lanes = slice(p * LANES, (p + 1) * LANES)
            vl = slice(v_col + p * LANES, v_col + (p + 1) * LANES)
            if s_all is not None:
                s_all[i, p, :, 0:LANES] = sf_ref[p].astype(BF16)
                s_all[nb_, p, :, LANES:2 * LANES] = sb_ref[p].astype(BF16)
            update(sf_ref, p, kz_ref[0, pl.ds(rf, CHUNK), lanes], v_ref[0, pl.ds(rf, CHUNK), vl],
                   cdf_ref[:, lanes])
            update(sb_ref, p, kz_ref[0, pl.ds(rb, CHUNK), RET_WIDTH + p * LANES:RET_WIDTH + (p + 1) * LANES],
                   v_ref[0, pl.ds(rb, CHUNK), vl], cdb_ref[:, lanes])
        return carry

    lax.fori_loop(0, n_chunks, body, 0, unroll=2 if n_chunks % 2 == 0 else 1)


def _retention_states_kernel(kv_ref, cdf_ref, cdb_ref, sf_out, sb_out, *, t):
    sf_out[...] = jnp.zeros(sf_out.shape, F32)
    sb_out[...] = jnp.zeros(sb_out.shape, F32)
    _state_sweep(kv_ref, kv_ref, 2 * RET_WIDTH, cdf_ref, cdb_ref, sf_out.at[0], sb_out.at[0], None,
                 t // CHUNK)


def _retention_states(kv, tables):
    b, t, _ = kv.shape
    state = jax.ShapeDtypeStruct((b, HEAD_PAIRS, LANES, LANES), F32)
    ins = [tables["cd_f"], tables["cd_b"]]
    return pl.pallas_call(
        functools.partial(_retention_states_kernel, t=t),
        out_shape=(state, state),
        grid=(b,),
        in_specs=[pl.BlockSpec((1, t, kv.shape[2]), lambda i: (i, 0, 0))] + [_resident(a.shape) for a in ins],
        out_specs=(pl.BlockSpec((1, HEAD_PAIRS, LANES, LANES), lambda i: (i, 0, 0, 0)),) * 2,
        compiler_params=_params(1),
        name="retention_states",
    )(kv, *ins)


def _mixers_kernel(*refs, t, has_init, out_states):
    it = iter(refs)
    q_ref, k_ref, v_ref, g_ref, kz_ref, pc_ref = (next(it) for _ in range(6))
    dcat_ref, xif_ref, xib_ref, cdf_ref, cdb_ref, cw_ref = (next(it) for _ in range(6))
    if has_init:
        if_ref, ib_ref = next(it), next(it)
    ret_ref, conv_ref = next(it), next(it)
    if out_states:
        sf_out, sb_out = next(it), next(it)
    sf_ref, sb_ref, s_all = next(it), next(it), next(it)

    n_chunks = t // CHUNK
    if has_init:
        sf_ref[...] = if_ref[0]
        sb_ref[...] = ib_ref[0]
    else:
        sf_ref[...] = jnp.zeros(sf_ref.shape, F32)
        sb_ref[...] = jnp.zeros(sb_ref.shape, F32)

    _state_sweep(kz_ref, v_ref, 0, cdf_ref, cdb_ref, sf_ref, sb_ref, s_all, n_chunks)
    if out_states:
        sf_out[0] = sf_ref[...]
        sb_out[0] = sb_ref[...]

    lane = lax.broadcasted_iota(jnp.int32, (1, LANES), 1)
    head0 = lane < HEAD_DIM
    m0 = jnp.where(head0, 1.0, 0.0).astype(BF16)
    m1 = jnp.where(head0, 0.0, 1.0).astype(BF16)
    w = cw_ref[...]

    def out_chunk(n, carry):
        r0 = pl.multiple_of(n * CHUNK, CHUNK)
        rows = pl.ds(r0, CHUNK)
        for p in range(HEAD_PAIRS):
            lanes = slice(p * LANES, (p + 1) * LANES)
            q = q_ref[0, rows, lanes]
            k = k_ref[0, rows, lanes]
            v = v_ref[0, rows, lanes]
            kk = jnp.concatenate([k * m0, k * m1], axis=0)
            vv = jnp.concatenate([v * m0, v * m1], axis=0)
            s = lax.dot_general(q, kk, (((1,), (1,)), ((), ())), preferred_element_type=F32)
            pr = (s * dcat_ref[p]).astype(BF16)
            o = _dot(pr, vv)
            c = _dot(q, s_all[n, p])
            o = o + c[:, 0:LANES] * xif_ref[:, lanes] + c[:, LANES:2 * LANES] * xib_ref[:, lanes]
            o2 = o * o
            s0 = jnp.sum(jnp.where(head0, o2, 0.0), axis=-1, keepdims=True)
            s1 = jnp.sum(jnp.where(head0, 0.0, o2), axis=-1, keepdims=True)
            r0n = lax.rsqrt(s0 * (1.0 / HEAD_DIM) + EPS)
            r1n = lax.rsqrt(s1 * (1.0 / HEAD_DIM) + EPS)
            o = o * jnp.where(head0, r0n, r1n)
            g = g_ref[0, rows, lanes].astype(F32)
            ret_ref[0, rows, lanes] = (o * (g * jax.nn.sigmoid(g))).astype(ret_ref.dtype)

        def gate_prod(start, size):
            blk = pc_ref[0, pl.ds(start, size), :].astype(F32)
            return blk[:, CONV_WIDTH:2 * CONV_WIDTH] * blk[:, 2 * CONV_WIDTH:3 * CONV_WIDTH]

        lo = pl.multiple_of(jnp.maximum(r0 - HALO, 0), HALO)
        hi = pl.multiple_of(jnp.minimum(r0 + CHUNK, t - HALO), HALO)
        before = jnp.where(n == 0, 0.0, gate_prod(lo, HALO))
        after = jnp.where(n == n_chunks - 1, 0.0, gate_prod(hi, HALO))
        mid = gate_prod(r0, CHUNK)
        ext = jnp.concatenate([before, mid, after], axis=0)
        acc = (ext[HALO - 1:HALO - 1 + CHUNK] * w[0:1, :] + mid * w[1:2, :]
               + ext[HALO + 1:HALO + 1 + CHUNK] * w[2:3, :])
        cb = pc_ref[0, rows, 0:CONV_WIDTH].astype(F32)
        conv_ref[0, rows, :] = (cb * acc).astype(conv_ref.dtype)
        return carry

    lax.fori_loop(0, n_chunks, out_chunk, 0, unroll=2 if n_chunks % 2 == 0 else 1)


def _mixers(qkvg, kz, pc, tables, conv_w, init, *, out_states):
    b, t, _ = qkvg.shape
    has_init = init is not None
    n_chunks = t // CHUNK
    col = lambda j: pl.BlockSpec((1, t, RET_WIDTH), lambda i, j=j: (i, 0, j))
    tabs = [tables[k] for k in ("dcat", "xi_f", "xi_b", "cd_f", "cd_b")] + [conv_w]
    in_specs = [col(0), col(1), col(2), col(3),
                pl.BlockSpec((1, t, kz.shape[2]), lambda i: (i, 0, 0)),
                pl.BlockSpec((1, t, pc.shape[2]), lambda i: (i, 0, 0))] + [_resident(a.shape) for a in tabs]
    args = [qkvg, qkvg, qkvg, qkvg, kz, pc] + tabs
    state_spec = pl.BlockSpec((1, HEAD_PAIRS, LANES, LANES), lambda i: (i, 0, 0, 0))
    if has_init:
        in_specs += [state_spec, state_spec]
        args += list(init)
    out_shape = [jax.ShapeDtypeStruct((b, t, RET_WIDTH), BF16),
                 jax.ShapeDtypeStruct((b, t, CONV_WIDTH), BF16)]
    out_specs = [pl.BlockSpec((1, t, RET_WIDTH), lambda i: (i, 0, 0)),
                 pl.BlockSpec((1, t, CONV_WIDTH), lambda i: (i, 0, 0))]
    if out_states:
        out_shape += [jax.ShapeDtypeStruct((b, HEAD_PAIRS, LANES, LANES), F32)] * 2
        out_specs += [state_spec, state_spec]
    return pl.pallas_call(
        functools.partial(_mixers_kernel, t=t, has_init=has_init, out_states=out_states),
        out_shape=tuple(out_shape),
        grid=(b,),
        in_specs=in_specs,
        out_specs=tuple(out_specs),
        scratch_shapes=[
            pltpu.VMEM((HEAD_PAIRS, LANES, LANES), F32),
            pltpu.VMEM((HEAD_PAIRS, LANES, LANES), F32),
            pltpu.VMEM((n_chunks, HEAD_PAIRS, LANES, 2 * LANES), BF16),
        ],
        compiler_params=_params(1),
        name="mixers",
    )(*args)


def _fourier_kernel(cm_ref, sm_ref, z_ref, o_ref, *, scale):
    zr = z_ref[0, :, 0:FOURIER_WIDTH]
    zi = z_ref[0, :, FOURIER_WIDTH:2 * FOURIER_WIDTH]
    o = _dot(cm_ref[...], zr) - _dot(sm_ref[...], zi)
    o_ref[0] = (o * scale).astype(o_ref.dtype)


def _fourier(z, cos_m, sin_m):
    b, t, _ = z.shape
    tb = min(t, 1024)
    scale = float(1.0 / np.sqrt(float(t) * FOURIER_GROUP_DIM))
    return pl.pallas_call(
        functools.partial(_fourier_kernel, scale=scale),
        out_shape=jax.ShapeDtypeStruct((b, t, FOURIER_WIDTH), BF16),
        grid=(t // tb, b),
        in_specs=[
            pl.BlockSpec((tb, t), lambda i, j: (i, 0)),
            pl.BlockSpec((tb, t), lambda i, j: (i, 0)),
            pl.BlockSpec((1, t, z.shape[2]), lambda i, j: (j, 0, 0)),
        ],
        out_specs=pl.BlockSpec((1, tb, FOURIER_WIDTH), lambda i, j: (j, i, 0)),
        compiler_params=_params(2),
        name="fourier",
    )(cos_m, sin_m, z)


def _out_proj_kernel(x_ref, mod_ref, ret_ref, four_ref, conv_ref, w_ref, o_ref, h_ref, *, d_model):
    nb, tt, d = x_ref.shape
    m = nb * tt
    acc = _dot(ret_ref[...].reshape(m, RET_WIDTH), w_ref[0:RET_WIDTH, :])
    acc = acc + _dot(four_ref[...].reshape(m, FOURIER_WIDTH), w_ref[RET_WIDTH:RET_WIDTH + FOURIER_WIDTH, :])
    acc = acc + _dot(conv_ref[...].reshape(m, CONV_WIDTH), w_ref[RET_WIDTH + FOURIER_WIDTH:, :])
    gate = mod_ref[0, :, 2 * d_model:3 * d_model]
    y = x_ref[...].reshape(m, d) + gate * acc
    o_ref[...] = y.reshape(nb, tt, d)
    shift = mod_ref[0, :, 3 * d_model:4 * d_model]
    scale = mod_ref[0, :, 4 * d_model:5 * d_model]
    h_ref[...] = _mod_norm(y, shift, scale).astype(h_ref.dtype).reshape(nb, tt, d)


def _out_proj(x, mod, ret, four, conv, w_out, *, nb, tt):
    b, t, d = x.shape
    shared_mod = mod.shape[0] == 1
    assert nb == 1 or shared_mod
    mod_map = (lambda i, j: (0, 0, 0)) if shared_mod else (lambda i, j: (i, 0, 0))
    blk = lambda w: pl.BlockSpec((nb, tt, w), lambda i, j: (i, j, 0))
    return pl.pallas_call(
        functools.partial(_out_proj_kernel, d_model=d),
        out_shape=(jax.ShapeDtypeStruct(x.shape, F32), jax.ShapeDtypeStruct(x.shape, BF16)),
        grid=(b // nb, t // tt),
        in_specs=[blk(d), pl.BlockSpec((1, 1, mod.shape[2]), mod_map),
                  blk(RET_WIDTH), blk(FOURIER_WIDTH), blk(CONV_WIDTH), _resident(w_out.shape)],
        out_specs=(blk(d), blk(d)),
        compiler_params=_params(2),
        name="out_proj",
    )(x, mod, ret, four, conv, w_out)


def _ffn_kernel(*refs, d_model, hidden, fh, halo, final_norm):
    it = iter(refs)
    x_ref, h_ref = next(it), next(it)
    hp_ref = hn_ref = None
    if halo:
        hp_ref, hn_ref = next(it), next(it)
    mod_ref, wu_ref, cw_ref, wd_ref = (next(it) for _ in range(4))
    fg_ref = next(it) if final_norm else None
    o_ref, he_ref, act_ref = next(it), next(it), next(it)

    _, tt, d = x_ref.shape
    i = pl.program_id(1)
    zeros = jnp.zeros((HALO, d), BF16)
    he_ref[0:HALO, :] = jnp.where(i == 0, zeros, hp_ref[0]) if halo else zeros
    he_ref[HALO:HALO + tt, :] = h_ref[0]
    he_ref[HALO + tt:, :] = jnp.where(i == pl.num_programs(1) - 1, zeros, hn_ref[0]) if halo else zeros

    def conv_rows(u, w):
        return (u[HALO - 1:HALO - 1 + tt] * w[0:1, :] + u[HALO:HALO + tt] * w[1:2, :]
                + u[HALO + 1:HALO + 1 + tt] * w[2:3, :])

    for j in range(hidden // fh):
        vc = slice(j * fh, (j + 1) * fh)
        gc = slice(hidden + j * fh, hidden + (j + 1) * fh)
        he = he_ref[...]
        val = conv_rows(_dot(he, wu_ref[:, vc]), cw_ref[:, vc])
        gate = conv_rows(_dot(he, wu_ref[:, gc]), cw_ref[:, gc])
        act_ref[:, vc] = (val * (gate * jax.nn.sigmoid(gate))).astype(BF16)

    gate2 = mod_ref[0, :, 5 * d_model:6 * d_model]
    y = x_ref[0] + gate2 * _dot(act_ref[...], wd_ref[...])
    if final_norm:
        ms = jnp.mean(y * y, axis=-1, keepdims=True)
        y = (y * lax.rsqrt(ms + EPS)) * fg_ref[...]
    o_ref[0] = y


def _ffn(x, h, mod, w_up, conv_w, w_down, final_g, *, tt, fh):
    b, t, d = x.shape
    hidden = w_down.shape[0]
    assert hidden % fh == 0 and t % tt == 0 and tt % HALO == 0
    halo = tt < t
    shared_mod = mod.shape[0] == 1
    mod_map = (lambda bi, i: (0, 0, 0)) if shared_mod else (lambda bi, i: (bi, 0, 0))
    final_norm = final_g is not None
    tile = pl.BlockSpec((1, tt, d), lambda bi, i: (bi, i, 0))
    in_specs = [tile, tile]
    args = [x, h]
    if halo:
        per = tt // HALO
        last = t // HALO - 1
        in_specs += [
            pl.BlockSpec((1, HALO, d), lambda bi, i: (bi, jnp.maximum(i * per - 1, 0), 0)),
            pl.BlockSpec((1, HALO, d), lambda bi, i: (bi, jnp.minimum((i + 1) * per, last), 0)),
        ]
        args += [h, h]
    in_specs += [pl.BlockSpec((1, 1, mod.shape[2]), mod_map),
                 _resident(w_up.shape), _resident(conv_w.shape), _resident(w_down.shape)]
    args += [mod, w_up, conv_w, w_down]
    if final_norm:
        in_specs.append(_resident((1, d)))
        args.append(final_g.reshape(1, d))
    return pl.pallas_call(
        functools.partial(_ffn_kernel, d_model=d, hidden=hidden, fh=fh, halo=halo, final_norm=final_norm),
        out_shape=jax.ShapeDtypeStruct(x.shape, F32),
        grid=(b, t // tt),
        in_specs=in_specs,
        out_specs=tile,
        scratch_shapes=[pltpu.VMEM((tt + 2 * HALO, d), BF16), pltpu.VMEM((tt, hidden), BF16)],
        compiler_params=_params(2),
        name="ffn",
    )(*args)


def _rope_tables(t):
    rows = t // GRID_W
    row = jnp.repeat(jnp.arange(rows, dtype=F32), GRID_W)
    col = jnp.tile(jnp.arange(GRID_W, dtype=F32), rows)
    n_freq = HEAD_DIM // 4
    freq = ROPE_BASE ** (-jnp.arange(n_freq, dtype=F32) / n_freq)
    ang = jnp.concatenate([row[:, None] * freq, col[:, None] * freq], axis=-1)
    reps = LANES // ang.shape[1]
    cos = jnp.tile(jnp.cos(ang), (1, reps))
    sign = jnp.where((jnp.arange(LANES) % HEAD_DIM) < HEAD_DIM // 2, -1.0, 1.0).astype(F32)
    sin = jnp.tile(jnp.sin(ang), (1, reps)) * sign
    return cos, sin


def _decay_tables(decay_logit):
    ld = jax.nn.log_sigmoid(decay_logit.astype(F32))
    ld_f, ld_b = ld[0], ld[1]
    i = jnp.arange(CHUNK, dtype=F32)
    lane_f = jnp.repeat(ld_f, HEAD_DIM)[None, :]
    lane_b = jnp.repeat(ld_b, HEAD_DIM)[None, :]
    diff = i[:, None] - i[None, :]
    dmask = jnp.where(diff > 0, jnp.exp(ld_f[:, None, None] * jnp.maximum(diff, 0.0)),
                      jnp.where(diff < 0, jnp.exp(ld_b[:, None, None] * jnp.maximum(-diff, 0.0)), 2.0))
    dcat = dmask.reshape(HEAD_PAIRS, 2, CHUNK, CHUNK).transpose(0, 2, 1, 3).reshape(HEAD_PAIRS, CHUNK, 2 * CHUNK)
    return {
        "dcat": dcat,
        "xi_f": jnp.exp(lane_f * (i[:, None] + 1.0)),
        "xi_b": jnp.exp(lane_b * (CHUNK - i[:, None])),
        "zeta_f": jnp.exp(lane_f * (CHUNK - 1.0 - i[:, None])),
        "zeta_b": jnp.exp(lane_b * i[:, None]),
        "cd_f": jnp.exp(lane_f * CHUNK),
        "cd_b": jnp.exp(lane_b * CHUNK),
    }


def _position_dft(t):
    idx = jnp.arange(t, dtype=jnp.int32)
    ang = ((idx[:, None] * idx[None, :]) % t).astype(F32) * (2.0 * np.pi / t)
    return jnp.cos(ang).astype(BF16), jnp.sin(ang).astype(BF16)


def _channel_dft():
    n = FOURIER_GROUP_DIM
    idx = jnp.arange(n, dtype=jnp.int32)
    ang = ((idx[:, None] * idx[None, :]) % n).astype(F32) * (2.0 * np.pi / n)
    eye = jnp.eye(FOURIER_GROUPS, dtype=F32)
    return jnp.concatenate([jnp.kron(eye, jnp.cos(ang)), jnp.kron(eye, jnp.sin(ang))], axis=1).astype(BF16)


def kernel(x, c, ctx, c_ctx, w_mod, b_mod, w_in, ret_decay_logit, mix_conv_w, w_out,
           ffn_w_up, ffn_conv_w, ffn_w_down, final_norm_g):
    b, t, d = x.shape
    _, lc, _ = ctx.shape
    depth = w_mod.shape[0]

    rows = ((b + 1 + 7) // 8) * 8
    c_all = jnp.zeros((rows, d), F32).at[:b].set(c).at[b].set(c_ctx)
    mod_all = _modulation(c_all, w_mod, b_mod)

    rope = _rope_tables(t)
    dft_c = _channel_dft()
    dft_x = _position_dft(t)
    dft_ctx = _position_dft(lc)

    tt = min(t, 512)
    tt_big = min(t, 1024)
    nb_ctx = max(1, min(b, 1024 // lc))
    fh = 256

    for l in range(depth):
        last = l == depth - 1
        mod_x = mod_all[l, :b].reshape(b, 1, 6 * d)
        mod_c = mod_all[l, b:b + 1].reshape(1, 1, 6 * d)
        w_in_l = w_in[l].astype(BF16)
        w_out_l = w_out[l].astype(BF16)
        w_up_l = ffn_w_up[l].astype(BF16)
        w_down_l = ffn_w_down[l].astype(BF16)
        tables = _decay_tables(ret_decay_logit[l])

        if last:
            (kv,) = _in_proj(ctx, mod_c, w_in_l, tables, None, dft_c, nb=nb_ctx, tt=lc, kv_only=True)
            st_f, st_b = _retention_states(kv, tables)
        else:
            qkvg_c, kz_c, z_c, pc_c = _in_proj(ctx, mod_c, w_in_l, tables, None, dft_c, nb=nb_ctx, tt=lc)
            ret_c, conv_c, st_f, st_b = _mixers(qkvg_c, kz_c, pc_c, tables, mix_conv_w[l], None,
                                                out_states=True)
            four_c = _fourier(z_c, *dft_ctx)

        qkvg, kz, z, pc = _in_proj(x, mod_x, w_in_l, tables, rope, dft_c, nb=1, tt=tt_big)
        ret, conv = _mixers(qkvg, kz, pc, tables, mix_conv_w[l], (st_f, st_b), out_states=False)
        four = _fourier(z, *dft_x)
        x, h = _out_proj(x, mod_x, ret, four, conv, w_out_l, nb=1, tt=tt)
        x = _ffn(x, h, mod_x, w_up_l, ffn_conv_w[l], w_down_l, final_norm_g if last else None, tt=tt_big, fh=fh)

        if not last:
            ctx, h_c = _out_proj(ctx, mod_c, ret_c, four_c, conv_c, w_out_l, nb=nb_ctx, tt=lc)
            ctx = _ffn(ctx, h_c, mod_c, w_up_l, ffn_conv_w[l], w_down_l, None, tt=lc, fh=fh)
    return x
```

```python
import functools

import jax
import jax.numpy as jnp
import numpy as np
from jax import lax
from jax.experimental import pallas as pl
from jax.experimental.pallas import tpu as pltpu

F32 = jnp.float32
BF16 = jnp.bfloat16

GRID_W = 64
RET_HEADS = 8
HEAD_DIM = 64
RET_WIDTH = RET_HEADS * HEAD_DIM
FOURIER_GROUPS = 4
FOURIER_GROUP_DIM = 64
FOURIER_WIDTH = FOURIER_GROUPS * FOURIER_GROUP_DIM
CONV_WIDTH = 256
ROPE_BASE = 10000.0
EPS = 1e-6
CHUNK = 128

LANES = 128
BF16_SUBLANES = 16
V7X_VMEM_LIMIT_BYTES = 60000 * 1024

HEAD_PAIRS = RET_WIDTH // LANES
HALO = BF16_SUBLANES

Q0 = 0
K0 = Q0 + RET_WIDTH
V0 = K0 + RET_WIDTH
G0 = V0 + RET_WIDTH
F0 = G0 + RET_WIDTH
CB0 = F0 + FOURIER_WIDTH
IN_WIDTH = CB0 + 3 * CONV_WIDTH


def _params(n_grid_dims, vmem=V7X_VMEM_LIMIT_BYTES):
    return pltpu.CompilerParams(
        dimension_semantics=("arbitrary",) * n_grid_dims,
        vmem_limit_bytes=vmem,
    )


def _dot(a, b):
    return jnp.dot(a, b, preferred_element_type=F32)


def _mod_norm(x, shift, scale):
    ms = jnp.mean(x * x, axis=-1, keepdims=True)
    return (x * lax.rsqrt(ms + EPS)) * (1.0 + scale) + shift


def _resident(shape):
    return pl.BlockSpec(shape, lambda *_: (0,) * len(shape), pipeline_mode=pl.Buffered(1))


def _mod_kernel(c_ref, w_ref, b_ref, o_ref):
    c = c_ref[...]
    s = c * jax.nn.sigmoid(c)
    o_ref[0] = jnp.dot(s, w_ref[0], preferred_element_type=F32,
                       precision=lax.Precision.HIGHEST) + b_ref[0]


def _modulation(c_all, w_mod, b_mod):
    depth, d, n = w_mod.shape
    rows = c_all.shape[0]
    tn = 1024
    return pl.pallas_call(
        _mod_kernel,
        out_shape=jax.ShapeDtypeStruct((depth, rows, n), F32),
        grid=(depth, n // tn),
        in_specs=[
            pl.BlockSpec((rows, d), lambda l, j: (0, 0)),
            pl.BlockSpec((1, d, tn), lambda l, j: (l, 0, j)),
            pl.BlockSpec((1, 1, tn), lambda l, j: (l, 0, j)),
        ],
        out_specs=pl.BlockSpec((1, rows, tn), lambda l, j: (l, 0, j)),
        compiler_params=_params(2),
        name="modulation",
    )(c_all, w_mod, b_mod.reshape(depth, 1, n))


def _rope_block(t, cos, sin, first_half):
    lo = pltpu.roll(t, LANES - HEAD_DIM // 2, 1)
    hi = pltpu.roll(t, HEAD_DIM // 2, 1)
    return t * cos + jnp.where(first_half, lo, hi) * sin


def _in_proj_kernel(*refs, d_model, use_rope, kv_only):
    it = iter(refs)
    x_ref, mod_ref, w_ref, zf_ref, zb_ref = (next(it) for _ in range(5))
    cos_ref = sin_ref = dft_ref = None
    if use_rope:
        cos_ref, sin_ref = next(it), next(it)
    if not kv_only:
        dft_ref = next(it)
    outs = list(it)

    nb, tt, d = x_ref.shape
    m = nb * tt
    x = x_ref[...].reshape(m, d)
    shift = mod_ref[0, :, 0:d_model]
    scale = mod_ref[0, :, d_model:2 * d_model]
    h = _mod_norm(x, shift, scale).astype(BF16)

    def put(ref, col, val):
        ref[:, :, col:col + val.shape[1]] = val.astype(ref.dtype).reshape(nb, tt, val.shape[1])

    if use_rope:
        cos = cos_ref[...]
        sin = sin_ref[...]
        lane = lax.broadcasted_iota(jnp.int32, (m, LANES), 1)
        first_half = (lane % HEAD_DIM) < (HEAD_DIM // 2)

    def rotary(t, j):
        tb = t[:, j * LANES:(j + 1) * LANES]
        return _rope_block(tb, cos, sin, first_half) if use_rope else tb

    def chunk_tiled(ref, j):
        tab = ref[:, j * LANES:(j + 1) * LANES]
        return jnp.concatenate([tab] * (m // CHUNK), axis=0)

    kv_ref = outs[0] if kv_only else outs[1]
    k = _dot(h, w_ref[:, K0:V0])
    for j in range(HEAD_PAIRS):
        kb = rotary(k, j) * (HEAD_DIM ** -0.5)
        if not kv_only:
            put(outs[0], RET_WIDTH + j * LANES, kb)
        put(kv_ref, j * LANES, kb * chunk_tiled(zf_ref, j))
        put(kv_ref, RET_WIDTH + j * LANES, kb * chunk_tiled(zb_ref, j))
    v = _dot(h, w_ref[:, V0:G0])
    if kv_only:
        put(kv_ref, 2 * RET_WIDTH, v)
        return

    qkvg_ref, _, z_ref, pc_ref = outs
    put(qkvg_ref, 2 * RET_WIDTH, v)
    q = _dot(h, w_ref[:, Q0:K0])
    for j in range(HEAD_PAIRS):
        put(qkvg_ref, j * LANES, rotary(q, j))
    put(qkvg_ref, 3 * RET_WIDTH, _dot(h, w_ref[:, G0:F0]))
    fpc = _dot(h, w_ref[:, F0:IN_WIDTH])
    f = fpc[:, 0:FOURIER_WIDTH].astype(BF16)
    put(z_ref, 0, _dot(f, dft_ref[...]))
    put(pc_ref, 0, fpc[:, FOURIER_WIDTH:])


def _in_proj(x, mod, w_in, tables, rope, dft_c, *, nb, tt, kv_only=False):
    b, t, d = x.shape
    use_rope = rope is not None
    shared_mod = mod.shape[0] == 1
    assert nb == 1 or (shared_mod and not use_rope)
    assert tt % CHUNK == 0
    mod_map = (lambda i, j: (0, 0, 0)) if shared_mod else (lambda i, j: (i, 0, 0))
    in_specs = [
        pl.BlockSpec((nb, tt, d), lambda i, j: (i, j, 0)),
        pl.BlockSpec((1, 1, mod.shape[2]), mod_map),
        _resident(w_in.shape),
        _resident(tables["zeta_f"].shape),
        _resident(tables["zeta_b"].shape),
    ]
    args = [x, mod, w_in, tables["zeta_f"], tables["zeta_b"]]
    if use_rope:
        in_specs += [pl.BlockSpec((tt, LANES), lambda i, j: (j, 0))] * 2
        args += list(rope)
    if kv_only:
        widths = (3 * RET_WIDTH,)
    else:
        in_specs.append(_resident(dft_c.shape))
        args.append(dft_c)
        widths = (4 * RET_WIDTH, 2 * RET_WIDTH, 2 * FOURIER_WIDTH, 3 * CONV_WIDTH)
    out_shape = tuple(jax.ShapeDtypeStruct((b, t, w), BF16) for w in widths)
    out_specs = tuple(pl.BlockSpec((nb, tt, w), lambda i, j: (i, j, 0)) for w in widths)
    return pl.pallas_call(
        functools.partial(_in_proj_kernel, d_model=d, use_rope=use_rope, kv_only=kv_only),
        out_shape=out_shape,
        grid=(b // nb, t // tt),
        in_specs=in_specs,
        out_specs=out_specs,
        compiler_params=_params(2),
        name="in_proj_kv" if kv_only else "in_proj",
    )(*args)


def _block_diag_mask():
    row = lax.broadcasted_iota(jnp.int32, (LANES, LANES), 0)
    col = lax.broadcasted_iota(jnp.int32, (LANES, LANES), 1)
    return (row < HEAD_DIM) == (col < HEAD_DIM)


def _state_sweep(kz_ref, v_ref, v_col, cdf_ref, cdb_ref, sf_ref, sb_ref, s_all, n_chunks):
    diag = _block_diag_mask()

    def update(state_ref, p, kz, v, decay):
        u = lax.dot_general(kz, v, (((0,), (0,)), ((), ())), preferred_element_type=F32)
        state_ref[p] = decay * state_ref[p] + jnp.where(diag, u, 0.0)

    def body(i, carry):
        rf = pl.multiple_of(i * CHUNK, CHUNK)
        nb_ = n_chunks - 1 - i
        rb = pl.multiple_of(nb_ * CHUNK, CHUNK)
        for p in range(HEAD_PAIRS):
            lanes = slice(p * LANES, (p + 1) * LANES)
            vl = slice(v_col + p * LANES, v_col + (p + 1) * LANES)
            if s_all is not None:
                s_all[i, p, :, 0:LANES] = sf_ref[p].astype(BF16)
                s_all[nb_, p, :, LANES:2 * LANES] = sb_ref[p].astype(BF16)
            update(sf_ref, p, kz_ref[0, pl.ds(rf, CHUNK), lanes], v_ref[0, pl.ds(rf, CHUNK), vl],
                   cdf_ref[:, lanes])
            update(sb_ref, p, kz_ref[0, pl.ds(rb, CHUNK), RET_WIDTH + p * LANES:RET_WIDTH + (p + 1) * LANES],
                   v_ref[0, pl.ds(rb, CHUNK), vl], cdb_ref[:, lanes])
        return carry

    lax.fori_loop(0, n_chunks, body, 0, unroll=2 if n_chunks % 2 == 0 else 1)


def _retention_states_kernel(kv_ref, cdf_ref, cdb_ref, sf_out, sb_out, *, t):
    sf_out[...] = jnp.zeros(sf_out.shape, F32)
    sb_out[...] = jnp.zeros(sb_out.shape, F32)
    _state_sweep(kv_ref, kv_ref, 2 * RET_WIDTH, cdf_ref, cdb_ref, sf_out.at[0], sb_out.at[0], None,
                 t // CHUNK)


def _retention_states(kv, tables):
    b, t, _ = kv.shape
    state = jax.ShapeDtypeStruct((b, HEAD_PAIRS, LANES, LANES), F32)
    ins = [tables["cd_f"], tables["cd_b"]]
    return pl.pallas_call(
        functools.partial(_retention_states_kernel, t=t),
        out_shape=(state, state),
        grid=(b,),
        in_specs=[pl.BlockSpec((1, t, kv.shape[2]), lambda i: (i, 0, 0))] + [_resident(a.shape) for a in ins],
        out_specs=(pl.BlockSpec((1, HEAD_PAIRS, LANES, LANES), lambda i: (i, 0, 0, 0)),) * 2,
        compiler_params=_params(1),
        name="retention_states",
    )(kv, *ins)


def _mixers_kernel(*refs, t, has_init, out_states):
    it = iter(refs)
    q_ref, k_ref, v_ref, g_ref, kz_ref, pc_ref = (next(it) for _ in range(6))
    dcat_ref, xif_ref, xib_ref, cdf_ref, cdb_ref, cw_ref = (next(it) for _ in range(6))
    if has_init:
        if_ref, ib_ref = next(it), next(it)
    ret_ref, conv_ref = next(it), next(it)
    if out_states:
        sf_out, sb_out = next(it), next(it)
    sf_ref, sb_ref, s_all = next(it), next(it), next(it)

    n_chunks = t // CHUNK
    if has_init:
        sf_ref[...] = if_ref[0]
        sb_ref[...] = ib_ref[0]
    else:
        sf_ref[...] = jnp.zeros(sf_ref.shape, F32)
        sb_ref[...] = jnp.zeros(sb_ref.shape, F32)

    _state_sweep(kz_ref, v_ref, 0, cdf_ref, cdb_ref, sf_ref, sb_ref, s_all, n_chunks)
    if out_states:
        sf_out[0] = sf_ref[...]
        sb_out[0] = sb_ref[...]

    lane = lax.broadcasted_iota(jnp.int32, (1, LANES), 1)
    head0 = lane < HEAD_DIM
    m0 = jnp.where(head0, 1.0, 0.0).astype(BF16)
    m1 = jnp.where(head0, 0.0, 1.0).astype(BF16)
    w = cw_ref[...]

    def out_chunk(n, carry):
        r0 = pl.multiple_of(n * CHUNK, CHUNK)
        rows = pl.ds(r0, CHUNK)
        for p in range(HEAD_PAIRS):
            lanes = slice(p * LANES, (p + 1) * LANES)
            q = q_ref[0, rows, lanes]
            k = k_ref[0, rows, lanes]
            v = v_ref[0, rows, lanes]
            kk = jnp.concatenate([k * m0, k * m1], axis=0)
            vv = jnp.concatenate([v * m0, v * m1], axis=0)
            s = lax.dot_general(q, kk, (((1,), (1,)), ((), ())), preferred_element_type=F32)
            pr = (s * dcat_ref[p]).astype(BF16)
            o = _dot(pr, vv)
            c = _dot(q, s_all[n, p])
            o = o + c[:, 0:LANES] * xif_ref[:, lanes] + c[:, LANES:2 * LANES] * xib_ref[:, lanes]
            o2 = o * o
            s0 = jnp.sum(jnp.where(head0, o2, 0.0), axis=-1, keepdims=True)
            s1 = jnp.sum(jnp.where(head0, 0.0, o2), axis=-1, keepdims=True)
            r0n = lax.rsqrt(s0 * (1.0 / HEAD_DIM) + EPS)
            r1n = lax.rsqrt(s1 * (1.0 / HEAD_DIM) + EPS)
            o = o * jnp.where(head0, r0n, r1n)
            g = g_ref[0, rows, lanes].astype(F32)
            ret_ref[0, rows, lanes] = (o * (g * jax.nn.sigmoid(g))).astype(ret_ref.dtype)

        def gate_prod(start, size):
            blk = pc_ref[0, pl.ds(start, size), :].astype(F32)
            return blk[:, CONV_WIDTH:2 * CONV_WIDTH] * blk[:, 2 * CONV_WIDTH:3 * CONV_WIDTH]

        lo = pl.multiple_of(jnp.maximum(r0 - HALO, 0), HALO)
        hi = pl.multiple_of(jnp.minimum(r0 + CHUNK, t - HALO), HALO)
        before = jnp.where(n == 0, 0.0, gate_prod(lo, HALO))
        after = jnp.where(n == n_chunks - 1, 0.0, gate_prod(hi, HALO))
        mid = gate_prod(r0, CHUNK)
        ext = jnp.concatenate([before, mid, after], axis=0)
        acc = (ext[HALO - 1:HALO - 1 + CHUNK] * w[0:1, :] + mid * w[1:2, :]
               + ext[HALO + 1:HALO + 1 + CHUNK] * w[2:3, :])
        cb = pc_ref[0, rows, 0:CONV_WIDTH].astype(F32)
        conv_ref[0, rows, :] = (cb * acc).astype(conv_ref.dtype)
        return carry

    lax.fori_loop(0, n_chunks, out_chunk, 0, unroll=2 if n_chunks % 2 == 0 else 1)


def _mixers(qkvg, kz, pc, tables, conv_w, init, *, out_states):
    b, t, _ = qkvg.shape
    has_init = init is not None
    n_chunks = t // CHUNK
    col = lambda j: pl.BlockSpec((1, t, RET_WIDTH), lambda i, j=j: (i, 0, j))
    tabs = [tables[k] for k in ("dcat", "xi_f", "xi_b", "cd_f", "cd_b")] + [conv_w]
    in_specs = [col(0), col(1), col(2), col(3),
                pl.BlockSpec((1, t, kz.shape[2]), lambda i: (i, 0, 0)),
                pl.BlockSpec((1, t, pc.shape[2]), lambda i: (i, 0, 0))] + [_resident(a.shape) for a in tabs]
    args = [qkvg, qkvg, qkvg, qkvg, kz, pc] + tabs
    state_spec = pl.BlockSpec((1, HEAD_PAIRS, LANES, LANES), lambda i: (i, 0, 0, 0))
    if has_init:
        in_specs += [state_spec, state_spec]
        args += list(init)
    out_shape = [jax.ShapeDtypeStruct((b, t, RET_WIDTH), BF16),
                 jax.ShapeDtypeStruct((b, t, CONV_WIDTH), BF16)]
    out_specs = [pl.BlockSpec((1, t, RET_WIDTH), lambda i: (i, 0, 0)),
                 pl.BlockSpec((1, t, CONV_WIDTH), lambda i: (i, 0, 0))]
    if out_states:
        out_shape += [jax.ShapeDtypeStruct((b, HEAD_PAIRS, LANES, LANES), F32)] * 2
        out_specs += [state_spec, state_spec]
    return pl.pallas_call(
        functools.partial(_mixers_kernel, t=t, has_init=has_init, out_states=out_states),
        out_shape=tuple(out_shape),
        grid=(b,),
        in_specs=in_specs,
        out_specs=tuple(out_specs),
        scratch_shapes=[
            pltpu.VMEM((HEAD_PAIRS, LANES, LANES), F32),
            pltpu.VMEM((HEAD_PAIRS, LANES, LANES), F32),
            pltpu.VMEM((n_chunks, HEAD_PAIRS, LANES, 2 * LANES), BF16),
        ],
        compiler_params=_params(1),
        name="mixers",
    )(*args)


def _out_proj_kernel(x_ref, mod_ref, ret_ref, conv_ref, z_ref, cm_ref, sm_ref, w_ref, o_ref, h_ref,
                     *, d_model, dft_scale):
    nb, tt, d = x_ref.shape
    m = nb * tt
    four = []
    for n in range(nb):
        zr = z_ref[n, :, 0:FOURIER_WIDTH]
        zi = z_ref[n, :, FOURIER_WIDTH:2 * FOURIER_WIDTH]
        four.append(((_dot(cm_ref[...], zr) - _dot(sm_ref[...], zi)) * dft_scale).astype(BF16))
    four = four[0] if nb == 1 else jnp.concatenate(four, axis=0)
    acc = _dot(ret_ref[...].reshape(m, RET_WIDTH), w_ref[0:RET_WIDTH, :])
    acc = acc + _dot(four, w_ref[RET_WIDTH:RET_WIDTH + FOURIER_WIDTH, :])
    acc = acc + _dot(conv_ref[...].reshape(m, CONV_WIDTH), w_ref[RET_WIDTH + FOURIER_WIDTH:, :])
    gate = mod_ref[0, :, 2 * d_model:3 * d_model]
    y = x_ref[...].reshape(m, d) + gate * acc
    o_ref[...] = y.reshape(nb, tt, d)
    shift = mod_ref[0, :, 3 * d_model:4 * d_model]
    scale = mod_ref[0, :, 4 * d_model:5 * d_model]
    h_ref[...] = _mod_norm(y, shift, scale).astype(h_ref.dtype).reshape(nb, tt, d)


def _out_proj(x, mod, ret, conv, z, cos_m, sin_m, w_out, *, nb, tt):
    b, t, d = x.shape
    shared_mod = mod.shape[0] == 1
    assert nb == 1 or (shared_mod and tt == t)
    mod_map = (lambda j, i: (0, 0, 0)) if shared_mod else (lambda j, i: (i, 0, 0))
    blk = lambda w: pl.BlockSpec((nb, tt, w), lambda j, i: (i, j, 0))
    dft_blk = pl.BlockSpec((tt, t), lambda j, i: (j, 0), pipeline_mode=pl.Buffered(1))
    dft_scale = float(1.0 / np.sqrt(float(t) * FOURIER_GROUP_DIM))
    return pl.pallas_call(
        functools.partial(_out_proj_kernel, d_model=d, dft_scale=dft_scale),
        out_shape=(jax.ShapeDtypeStruct(x.shape, F32), jax.ShapeDtypeStruct(x.shape, BF16)),
        grid=(t // tt, b // nb),
        in_specs=[blk(d), pl.BlockSpec((1, 1, mod.shape[2]), mod_map),
                  blk(RET_WIDTH), blk(CONV_WIDTH),
                  pl.BlockSpec((nb, t, z.shape[2]), lambda j, i: (i, 0, 0)),
                  dft_blk, dft_blk, _resident(w_out.shape)],
        out_specs=(blk(d), blk(d)),
        compiler_params=_params(2),
        name="out_proj",
    )(x, mod, ret, conv, z, cos_m, sin_m, w_out)


def _ffn_kernel(*refs, d_model, hidden, fh, halo, final_norm):
    it = iter(refs)
    x_ref, h_ref = next(it), next(it)
    hp_ref = hn_ref = None
    if halo:
        hp_ref, hn_ref = next(it), next(it)
    mod_ref, wu_ref, cw_ref, wd_ref = (next(it) for _ in range(4))
    fg_ref = next(it) if final_norm else None
    o_ref, he_ref, act_ref = next(it), next(it), next(it)

    _, tt, d = x_ref.shape
    i = pl.program_id(1)
    zeros = jnp.zeros((HALO, d), BF16)
    he_ref[0:HALO, :] = jnp.where(i == 0, zeros, hp_ref[0]) if halo else zeros
    he_ref[HALO:HALO + tt, :] = h_ref[0]
    he_ref[HALO + tt:, :] = jnp.where(i == pl.num_programs(1) - 1, zeros, hn_ref[0]) if halo else zeros

    def conv_rows(u, w):
        return (u[HALO - 1:HALO - 1 + tt] * w[0:1, :] + u[HALO:HALO + tt] * w[1:2, :]
                + u[HALO + 1:HALO + 1 + tt] * w[2:3, :])

    for j in range(hidden // fh):
        vc = slice(j * fh, (j + 1) * fh)
        gc = slice(hidden + j * fh, hidden + (j + 1) * fh)
        he = he_ref[...]
        val = conv_rows(_dot(he, wu_ref[:, vc]), cw_ref[:, vc])
        gate = conv_rows(_dot(he, wu_ref[:, gc]), cw_ref[:, gc])
        act_ref[:, vc] = (val * (gate * jax.nn.sigmoid(gate))).astype(BF16)

    gate2 = mod_ref[0, :, 5 * d_model:6 * d_model]
    y = x_ref[0] + gate2 * _dot(act_ref[...], wd_ref[...])
    if final_norm:
        ms = jnp.mean(y * y, axis=-1, keepdims=True)
        y = (y * lax.rsqrt(ms + EPS)) * fg_ref[...]
    o_ref[0] = y


def _ffn(x, h, mod, w_up, conv_w, w_down, final_g, *, tt, fh):
    b, t, d = x.shape
    hidden = w_down.shape[0]
    assert hidden % fh == 0 and t % tt == 0 and tt % HALO == 0
    halo = tt < t
    shared_mod = mod.shape[0] == 1
    mod_map = (lambda bi, i: (0, 0, 0)) if shared_mod else (lambda bi, i: (bi, 0, 0))
    final_norm = final_g is not None
    tile = pl.BlockSpec((1, tt, d), lambda bi, i: (bi, i, 0))
    in_specs = [tile, tile]
    args = [x, h]
    if halo:
        per = tt // HALO
        last = t // HALO - 1
        in_specs += [
            pl.BlockSpec((1, HALO, d), lambda bi, i: (bi, jnp.maximum(i * per - 1, 0), 0)),
            pl.BlockSpec((1, HALO, d), lambda bi, i: (bi, jnp.minimum((i + 1) * per, last), 0)),
        ]
        args += [h, h]
    in_specs += [pl.BlockSpec((1, 1, mod.shape[2]), mod_map),
                 _resident(w_up.shape), _resident(conv_w.shape), _resident(w_down.shape)]
    args += [mod, w_up, conv_w, w_down]
    if final_norm:
        in_specs.append(_resident((1, d)))
        args.append(final_g.reshape(1, d))
    return pl.pallas_call(
        functools.partial(_ffn_kernel, d_model=d, hidden=hidden, fh=fh, halo=halo, final_norm=final_norm),
        out_shape=jax.ShapeDtypeStruct(x.shape, F32),
        grid=(b, t // tt),
        in_specs=in_specs,
        out_specs=tile,
        scratch_shapes=[pltpu.VMEM((tt + 2 * HALO, d), BF16), pltpu.VMEM((tt, hidden), BF16)],
        compiler_params=_params(2),
        name="ffn",
    )(*args)


def _rope_tables(t):
    rows = t // GRID_W
    row = jnp.repeat(jnp.arange(rows, dtype=F32), GRID_W)
    col = jnp.tile(jnp.arange(GRID_W, dtype=F32), rows)
    n_freq = HEAD_DIM // 4
    freq = ROPE_BASE ** (-jnp.arange(n_freq, dtype=F32) / n_freq)
    ang = jnp.concatenate([row[:, None] * freq, col[:, None] * freq], axis=-1)
    reps = LANES // ang.shape[1]
    cos = jnp.tile(jnp.cos(ang), (1, reps))
    sign = jnp.where((jnp.arange(LANES) % HEAD_DIM) < HEAD_DIM // 2, -1.0, 1.0).astype(F32)
    sin = jnp.tile(jnp.sin(ang), (1, reps)) * sign
    return cos, sin


def _decay_tables(decay_logit):
    ld = jax.nn.log_sigmoid(decay_logit.astype(F32))
    ld_f, ld_b = ld[0], ld[1]
    i = jnp.arange(CHUNK, dtype=F32)
    lane_f = jnp.repeat(ld_f, HEAD_DIM)[None, :]
    lane_b = jnp.repeat(ld_b, HEAD_DIM)[None, :]
    diff = i[:, None] - i[None, :]
    dmask = jnp.where(diff > 0, jnp.exp(ld_f[:, None, None] * jnp.maximum(diff, 0.0)),
                      jnp.where(diff < 0, jnp.exp(ld_b[:, None, None] * jnp.maximum(-diff, 0.0)), 2.0))
    dcat = dmask.reshape(HEAD_PAIRS, 2, CHUNK, CHUNK).transpose(0, 2, 1, 3).reshape(HEAD_PAIRS, CHUNK, 2 * CHUNK)
    return {
        "dcat": dcat,
        "xi_f": jnp.exp(lane_f * (i[:, None] + 1.0)),
        "xi_b": jnp.exp(lane_b * (CHUNK - i[:, None])),
        "zeta_f": jnp.exp(lane_f * (CHUNK - 1.0 - i[:, None])),
        "zeta_b": jnp.exp(lane_b * i[:, None]),
        "cd_f": jnp.exp(lane_f * CHUNK),
        "cd_b": jnp.exp(lane_b * CHUNK),
    }


def _position_dft(t):
    blk = GRID_W if t % GRID_W == 0 else 1
    r = jnp.arange(t, dtype=jnp.int32)[:, None]

    def table(cols):
        ang = ((r * cols[None, :]) % t).astype(F32) * (2.0 * np.pi / t)
        return jnp.cos(ang)[:, :, None], jnp.sin(ang)[:, :, None]

    ca, sa = table(jnp.arange(t // blk, dtype=jnp.int32) * blk)
    cb, sb = table(jnp.arange(blk, dtype=jnp.int32))
    cb, sb = cb.reshape(t, 1, blk), sb.reshape(t, 1, blk)
    cos = (ca * cb - sa * sb).reshape(t, t)
    sin = (sa * cb + ca * sb).reshape(t, t)
    return cos.astype(BF16), sin.astype(BF16)


def _channel_dft():
    n = FOURIER_GROUP_DIM
    idx = jnp.arange(n, dtype=jnp.int32)
    ang = ((idx[:, None] * idx[None, :]) % n).astype(F32) * (2.0 * np.pi / n)
    eye = jnp.eye(FOURIER_GROUPS, dtype=F32)
    return jnp.concatenate([jnp.kron(eye, jnp.cos(ang)), jnp.kron(eye, jnp.sin(ang))], axis=1).astype(BF16)


def kernel(x, c, ctx, c_ctx, w_mod, b_mod, w_in, ret_decay_logit, mix_conv_w, w_out,
           ffn_w_up, ffn_conv_w, ffn_w_down, final_norm_g):
    b, t, d = x.shape
    _, lc, _ = ctx.shape
    depth = w_mod.shape[0]

    rows = ((b + 1 + 7) // 8) * 8
    c_all = jnp.zeros((rows, d), F32).at[:b].set(c).at[b].set(c_ctx)
    mod_all = _modulation(c_all, w_mod, b_mod)

    rope = _rope_tables(t)
    dft_c = _channel_dft()
    dft_x = _position_dft(t)
    dft_ctx = _position_dft(lc)

    tt = min(t, 512)
    tt_big = min(t, 1024)
    nb_ctx = max(1, min(b, 1024 // lc))
    fh = 256

    for l in range(depth):
        last = l == depth - 1
        mod_x = mod_all[l, :b].reshape(b, 1, 6 * d)
        mod_c = mod_all[l, b:b + 1].reshape(1, 1, 6 * d)
        w_in_l = w_in[l].astype(BF16)
        w_out_l = w_out[l].astype(BF16)
        w_up_l = ffn_w_up[l].astype(BF16)
        w_down_l = ffn_w_down[l].astype(BF16)
        tables = _decay_tables(ret_decay_logit[l])

        if last:
            (kv,) = _in_proj(ctx, mod_c, w_in_l, tables, None, dft_c, nb=nb_ctx, tt=lc, kv_only=True)
            st_f, st_b = _retention_states(kv, tables)
        else:
            qkvg_c, kz_c, z_c, pc_c = _in_proj(ctx, mod_c, w_in_l, tables, None, dft_c, nb=nb_ctx, tt=lc)
            ret_c, conv_c, st_f, st_b = _mixers(qkvg_c, kz_c, pc_c, tables, mix_conv_w[l], None,
                                                out_states=True)

        qkvg, kz, z, pc = _in_proj(x, mod_x, w_in_l, tables, rope, dft_c, nb=1, tt=tt_big)
        ret, conv = _mixers(qkvg, kz, pc, tables, mix_conv_w[l], (st_f, st_b), out_states=False)
        x, h = _out_proj(x, mod_x, ret, conv, z, *dft_x, w_out_l, nb=1, tt=tt_big)
        x = _ffn(x, h, mod_x, w_up_l, ffn_conv_w[l], w_down_l, final_norm_g if last else None, tt=tt_big, fh=fh)

        if not last:
            ctx, h_c = _out_proj(ctx, mod_c, ret_c, conv_c, z_c, *dft_ctx, w_out_l, nb=nb_ctx, tt=lc)
            ctx = _ffn(ctx, h_c, mod_c, w_up_l, ffn_conv_w[l], w_down_l, None, tt=lc, fh=fh)
    return x
```

```python
import functools

import jax
import jax.numpy as jnp
import numpy as np
from jax import lax
from jax.experimental import pallas as pl
from jax.experimental.pallas import tpu as pltpu

F32 = jnp.float32
BF16 = jnp.bfloat16

GRID_W = 64
RET_HEADS = 8
HEAD_DIM = 64
RET_WIDTH = RET_HEADS * HEAD_DIM
FOURIER_GROUPS = 4
FOURIER_GROUP_DIM = 64
FOURIER_WIDTH = FOURIER_GROUPS * FOURIER_GROUP_DIM
CONV_WIDTH = 256
ROPE_BASE = 10000.0
EPS = 1e-6
CHUNK = 128

LANES = 128
BF16_SUBLANES = 16
V7X_VMEM_LIMIT_BYTES = 60000 * 1024

HEAD_PAIRS = RET_WIDTH // LANES
HALO = BF16_SUBLANES
OUT_PROJ_ROWS = 512

Q0 = 0
K0 = Q0 + RET_WIDTH
V0 = K0 + RET_WIDTH
G0 = V0 + RET_WIDTH
F0 = G0 + RET_WIDTH
CB0 = F0 + FOURIER_WIDTH
IN_WIDTH = CB0 + 3 * CONV_WIDTH


def _params(n_grid_dims, vmem=V7X_VMEM_LIMIT_BYTES):
    return pltpu.CompilerParams(
        dimension_semantics=("arbitrary",) * n_grid_dims,
        vmem_limit_bytes=vmem,
    )


def _dot(a, b):
    return jnp.dot(a, b, preferred_element_type=F32)


def _mod_norm(x, shift, scale):
    ms = jnp.mean(x * x, axis=-1, keepdims=True)
    return (x * lax.rsqrt(ms + EPS)) * (1.0 + scale) + shift


def _resident(shape):
    return pl.BlockSpec(shape, lambda *_: (0,) * len(shape), pipeline_mode=pl.Buffered(1))


def _mod_kernel(c_ref, w_ref, b_ref, o_ref):
    c = c_ref[...]
    s = c * jax.nn.sigmoid(c)
    o_ref[0] = jnp.dot(s, w_ref[0], preferred_element_type=F32,
                       precision=lax.Precision.HIGHEST) + b_ref[0]


def _modulation(c_all, w_mod, b_mod):
    depth, d, n = w_mod.shape
    rows = c_all.shape[0]
    tn = 1024
    return pl.pallas_call(
        _mod_kernel,
        out_shape=jax.ShapeDtypeStruct((depth, rows, n), F32),
        grid=(depth, n // tn),
        in_specs=[
            pl.BlockSpec((rows, d), lambda l, j: (0, 0)),
            pl.BlockSpec((1, d, tn), lambda l, j: (l, 0, j)),
            pl.BlockSpec((1, 1, tn), lambda l, j: (l, 0, j)),
        ],
        out_specs=pl.BlockSpec((1, rows, tn), lambda l, j: (l, 0, j)),
        compiler_params=_params(2),
        name="modulation",
    )(c_all, w_mod, b_mod.reshape(depth, 1, n))


def _rope_block(t, cos, sin, first_half):
    lo = pltpu.roll(t, LANES - HEAD_DIM // 2, 1)
    hi = pltpu.roll(t, HEAD_DIM // 2, 1)
    return t * cos + jnp.where(first_half, lo, hi) * sin


def _in_proj_kernel(*refs, d_model, use_rope, kv_only):
    it = iter(refs)
    x_ref, mod_ref, w_ref, zf_ref, zb_ref = (next(it) for _ in range(5))
    cos_ref = sin_ref = dft_ref = None
    if use_rope:
        cos_ref, sin_ref = next(it), next(it)
    if not kv_only:
        dft_ref = next(it)
    outs = list(it)

    nb, tt, d = x_ref.shape
    m = nb * tt
    x = x_ref[...].reshape(m, d)
    shift = mod_ref[0, :, 0:d_model]
    scale = mod_ref[0, :, d_model:2 * d_model]
    h = _mod_norm(x, shift, scale).astype(BF16)

    def put(ref, col, val):
        ref[:, :, col:col + val.shape[1]] = val.astype(ref.dtype).reshape(nb, tt, val.shape[1])

    if use_rope:
        cos = cos_ref[...]
        sin = sin_ref[...]
        lane = lax.broadcasted_iota(jnp.int32, (m, LANES), 1)
        first_half = (lane % HEAD_DIM) < (HEAD_DIM // 2)

    def rotary(t, j):
        tb = t[:, j * LANES:(j + 1) * LANES]
        return _rope_block(tb, cos, sin, first_half) if use_rope else tb

    def chunk_tiled(ref, j):
        tab = ref[:, j * LANES:(j + 1) * LANES]
        return jnp.concatenate([tab] * (m // CHUNK), axis=0)

    kv_ref = outs[0] if kv_only else outs[1]
    k = _dot(h, w_ref[:, K0:V0])
    for j in range(HEAD_PAIRS):
        kb = rotary(k, j) * (HEAD_DIM ** -0.5)
        if not kv_only:
            put(outs[0], RET_WIDTH + j * LANES, kb)
        put(kv_ref, j * LANES, kb * chunk_tiled(zf_ref, j))
        put(kv_ref, RET_WIDTH + j * LANES, kb * chunk_tiled(zb_ref, j))
    v = _dot(h, w_ref[:, V0:G0])
    if kv_only:
        put(kv_ref, 2 * RET_WIDTH, v)
        return

    qkvg_ref, _, z_ref, pc_ref = outs
    put(qkvg_ref, 2 * RET_WIDTH, v)
    q = _dot(h, w_ref[:, Q0:K0])
    for j in range(HEAD_PAIRS):
        put(qkvg_ref, j * LANES, rotary(q, j))
    g = _dot(h, w_ref[:, G0:F0])
    put(qkvg_ref, 3 * RET_WIDTH, g * jax.nn.sigmoid(g))
    fpc = _dot(h, w_ref[:, F0:IN_WIDTH])
    f = fpc[:, 0:FOURIER_WIDTH].astype(BF16)
    put(z_ref, 0, _dot(f, dft_ref[...]))
    put(pc_ref, 0, fpc[:, CB0 - F0:CB0 - F0 + CONV_WIDTH])
    put(pc_ref, CONV_WIDTH, fpc[:, CB0 - F0 + CONV_WIDTH:CB0 - F0 + 2 * CONV_WIDTH]
        * fpc[:, CB0 - F0 + 2 * CONV_WIDTH:])


def _in_proj(x, mod, w_in, tables, rope, dft_c, *, nb, tt, kv_only=False):
    b, t, d = x.shape
    use_rope = rope is not None
    shared_mod = mod.shape[0] == 1
    assert nb == 1 or (shared_mod and not use_rope)
    assert tt % CHUNK == 0
    mod_map = (lambda i, j: (0, 0, 0)) if shared_mod else (lambda i, j: (i, 0, 0))
    in_specs = [
        pl.BlockSpec((nb, tt, d), lambda i, j: (i, j, 0)),
        pl.BlockSpec((1, 1, mod.shape[2]), mod_map),
        _resident(w_in.shape),
        _resident(tables["zeta_f"].shape),
        _resident(tables["zeta_b"].shape),
    ]
    args = [x, mod, w_in, tables["zeta_f"], tables["zeta_b"]]
    if use_rope:
        in_specs += [pl.BlockSpec((tt, LANES), lambda i, j: (j, 0))] * 2
        args += list(rope)
    if kv_only:
        widths = (3 * RET_WIDTH,)
    else:
        in_specs.append(_resident(dft_c.shape))
        args.append(dft_c)
        widths = (4 * RET_WIDTH, 2 * RET_WIDTH, 2 * FOURIER_WIDTH, 2 * CONV_WIDTH)
    out_shape = tuple(jax.ShapeDtypeStruct((b, t, w), BF16) for w in widths)
    out_specs = tuple(pl.BlockSpec((nb, tt, w), lambda i, j: (i, j, 0)) for w in widths)
    return pl.pallas_call(
        functools.partial(_in_proj_kernel, d_model=d, use_rope=use_rope, kv_only=kv_only),
        out_shape=out_shape,
        grid=(b // nb, t // tt),
        in_specs=in_specs,
        out_specs=out_specs,
        compiler_params=_params(2),
        name="in_proj_kv" if kv_only else "in_proj",
    )(*args)


def _block_diag_mask():
    row = lax.broadcasted_iota(jnp.int32, (LANES, LANES), 0)
    col = lax.broadcasted_iota(jnp.int32, (LANES, LANES), 1)
    return (row < HEAD_DIM) == (col < HEAD_DIM)


def _state_sweep(kz_ref, v_ref, v_col, cdf_ref, cdb_ref, sf_ref, sb_ref, s_all, n_chunks):
    diag = _block_diag_mask()

    def update(state_ref, p, kz, v, decay):
        u = lax.dot_general(kz, v, (((0,), (0,)), ((), ())), preferred_element_type=F32)
        state_ref[p] = decay * state_ref[p] + jnp.where(diag, u, 0.0)

    def body(i, carry):
        rf = pl.multiple_of(i * CHUNK, CHUNK)
        nb_ = n_chunks - 1 - i
        rb = pl.multiple_of(nb_ * CHUNK, CHUNK)
        for p in range(HEAD_PAIRS):
            lanes = slice(p * LANES, (p + 1) * LANES)
            vl = slice(v_col + p * LANES, v_col + (p + 1) * LANES)
            if s_all is not None:
                s_all[i, p, :, 0:LANES] = sf_ref[p].astype(BF16)
                s_all[nb_, p, :, LANES:2 * LANES] = sb_ref[p].astype(BF16)
            update(sf_ref, p, kz_ref[0, pl.ds(rf, CHUNK), lanes], v_ref[0, pl.ds(rf, CHUNK), vl],
                   cdf_ref[:, lanes])
            update(sb_ref, p, kz_ref[0, pl.ds(rb, CHUNK), RET_WIDTH + p * LANES:RET_WIDTH + (p + 1) * LANES],
                   v_ref[0, pl.ds(rb, CHUNK), vl], cdb_ref[:, lanes])
        return carry

    lax.fori_loop(0, n_chunks, body, 0, unroll=2 if n_chunks % 2 == 0 else 1)


def _retention_states_kernel(kv_ref, cdf_ref, cdb_ref, sf_out, sb_out, *, t):
    sf_out[...] = jnp.zeros(sf_out.shape, F32)
    sb_out[...] = jnp.zeros(sb_out.shape, F32)
    _state_sweep(kv_ref, kv_ref, 2 * RET_WIDTH, cdf_ref, cdb_ref, sf_out.at[0], sb_out.at[0], None,
                 t // CHUNK)


def _retention_states(kv, tables):
    b, t, _ = kv.shape
    state = jax.ShapeDtypeStruct((b, HEAD_PAIRS, LANES, LANES), F32)
    ins = [tables["cd_f"], tables["cd_b"]]
    return pl.pallas_call(
        functools.partial(_retention_states_kernel, t=t),
        out_shape=(state, state),
        grid=(b,),
        in_specs=[pl.BlockSpec((1, t, kv.shape[2]), lambda i: (i, 0, 0))] + [_resident(a.shape) for a in ins],
        out_specs=(pl.BlockSpec((1, HEAD_PAIRS, LANES, LANES), lambda i: (i, 0, 0, 0)),) * 2,
        compiler_params=_params(1),
        name="retention_states",
    )(kv, *ins)


def _mixers_kernel(*refs, t, has_init, out_states):
    it = iter(refs)
    q_ref, k_ref, v_ref, g_ref, kz_ref, pc_ref = (next(it) for _ in range(6))
    dcat_ref, xif_ref, xib_ref, cdf_ref, cdb_ref, cw_ref = (next(it) for _ in range(6))
    if has_init:
        if_ref, ib_ref = next(it), next(it)
    ret_ref, conv_ref = next(it), next(it)
    if out_states:
        sf_out, sb_out = next(it), next(it)
    sf_ref, sb_ref, s_all = next(it), next(it), next(it)

    n_chunks = t // CHUNK
    if has_init:
        sf_ref[...] = if_ref[0]
        sb_ref[...] = ib_ref[0]
    else:
        sf_ref[...] = jnp.zeros(sf_ref.shape, F32)
        sb_ref[...] = jnp.zeros(sb_ref.shape, F32)

    _state_sweep(kz_ref, v_ref, 0, cdf_ref, cdb_ref, sf_ref, sb_ref, s_all, n_chunks)
    if out_states:
        sf_out[0] = sf_ref[...]
        sb_out[0] = sb_ref[...]

    lane = lax.broadcasted_iota(jnp.int32, (1, LANES), 1)
    head0 = lane < HEAD_DIM
    m0 = jnp.where(head0, 1.0, 0.0).astype(BF16)
    m1 = jnp.where(head0, 0.0, 1.0).astype(BF16)
    w = cw_ref[...]
    a0 = jnp.where(head0, 1.0 / HEAD_DIM, 0.0)
    a1 = jnp.where(head0, 0.0, 1.0 / HEAD_DIM)

    def out_chunk(n, carry):
        r0 = pl.multiple_of(n * CHUNK, CHUNK)
        rows = pl.ds(r0, CHUNK)
        for p in range(HEAD_PAIRS):
            lanes = slice(p * LANES, (p + 1) * LANES)
            q = q_ref[0, rows, lanes]
            k = k_ref[0, rows, lanes]
            v = v_ref[0, rows, lanes]
            kk = jnp.concatenate([k * m0, k * m1], axis=0)
            vv = jnp.concatenate([v * m0, v * m1], axis=0)
            s = lax.dot_general(q, kk, (((1,), (1,)), ((), ())), preferred_element_type=F32)
            pr = (s * dcat_ref[p]).astype(BF16)
            o = _dot(pr, vv)
            c = _dot(q, s_all[n, p])
            o = o + c[:, 0:LANES] * xif_ref[:, lanes] + c[:, LANES:2 * LANES] * xib_ref[:, lanes]
            o2 = o * o + EPS
            r0n = lax.rsqrt(jnp.sum(o2 * a0, axis=-1, keepdims=True))
            r1n = lax.rsqrt(jnp.sum(o2 * a1, axis=-1, keepdims=True))
            o = o * jnp.where(head0, r0n, r1n)
            ret_ref[0, rows, lanes] = (o * g_ref[0, rows, lanes].astype(F32)).astype(ret_ref.dtype)

        def gate_prod(start, size):
            return pc_ref[0, pl.ds(start, size), CONV_WIDTH:2 * CONV_WIDTH].astype(F32)

        lo = pl.multiple_of(jnp.maximum(r0 - HALO, 0), HALO)
        hi = pl.multiple_of(jnp.minimum(r0 + CHUNK, t - HALO), HALO)
        before = jnp.where(n == 0, 0.0, gate_prod(lo, HALO))
        after = jnp.where(n == n_chunks - 1, 0.0, gate_prod(hi, HALO))
        mid = gate_prod(r0, CHUNK)
        ext = jnp.concatenate([before, mid, after], axis=0)
        acc = (ext[HALO - 1:HALO - 1 + CHUNK] * w[0:1, :] + mid * w[1:2, :]
               + ext[HALO + 1:HALO + 1 + CHUNK] * w[2:3, :])
        cb = pc_ref[0, rows, 0:CONV_WIDTH].astype(F32)
        conv_ref[0, rows, :] = (cb * acc).astype(conv_ref.dtype)
        return carry

    lax.fori_loop(0, n_chunks, out_chunk, 0, unroll=8 if n_chunks % 8 == 0 else 2)


def _mixers(qkvg, kz, pc, tables, conv_w, init, *, out_states):
    b, t, _ = qkvg.shape
    has_init = init is not None
    n_chunks = t // CHUNK
    col = lambda j: pl.BlockSpec((1, t, RET_WIDTH), lambda i, j=j: (i, 0, j))
    tabs = [tables[k] for k in ("dcat", "xi_f", "xi_b", "cd_f", "cd_b")] + [conv_w]
    in_specs = [col(0), col(1), col(2), col(3),
                pl.BlockSpec((1, t, kz.shape[2]), lambda i: (i, 0, 0)),
                pl.BlockSpec((1, t, pc.shape[2]), lambda i: (i, 0, 0))] + [_resident(a.shape) for a in tabs]
    args = [qkvg, qkvg, qkvg, qkvg, kz, pc] + tabs
    state_spec = pl.BlockSpec((1, HEAD_PAIRS, LANES, LANES), lambda i: (i, 0, 0, 0))
    if has_init:
        in_specs += [state_spec, state_spec]
        args += list(init)
    out_shape = [jax.ShapeDtypeStruct((b, t, RET_WIDTH), BF16),
                 jax.ShapeDtypeStruct((b, t, CONV_WIDTH), BF16)]
    out_specs = [pl.BlockSpec((1, t, RET_WIDTH), lambda i: (i, 0, 0)),
                 pl.BlockSpec((1, t, CONV_WIDTH), lambda i: (i, 0, 0))]
    if out_states:
        out_shape += [jax.ShapeDtypeStruct((b, HEAD_PAIRS, LANES, LANES), F32)] * 2
        out_specs += [state_spec, state_spec]
    return pl.pallas_call(
        functools.partial(_mixers_kernel, t=t, has_init=has_init, out_states=out_states),
        out_shape=tuple(out_shape),
        grid=(b,),
        in_specs=in_specs,
        out_specs=tuple(out_specs),
        scratch_shapes=[
            pltpu.VMEM((HEAD_PAIRS, LANES, LANES), F32),
            pltpu.VMEM((HEAD_PAIRS, LANES, LANES), F32),
            pltpu.VMEM((n_chunks, HEAD_PAIRS, LANES, 2 * LANES), BF16),
        ],
        compiler_params=_params(1),
        name="mixers",
    )(*args)


def _out_proj_kernel(x_ref, mod_ref, ret_ref, conv_ref, z_ref, cm_ref, sm_ref, w_ref, o_ref, h_ref,
                     *, d_model, dft_scale):
    nb, tt, d = x_ref.shape
    gate = mod_ref[0, :, 2 * d_model:3 * d_model]
    shift = mod_ref[0, :, 3 * d_model:4 * d_model]
    scale = mod_ref[0, :, 4 * d_model:5 * d_model]
    rb = min(tt, OUT_PROJ_ROWS)
    for n in range(nb):
        zr = z_ref[n, :, 0:FOURIER_WIDTH]
        zi = z_ref[n, :, FOURIER_WIDTH:2 * FOURIER_WIDTH]
        for r in range(tt // rb):
            rows = slice(r * rb, (r + 1) * rb)
            four = ((_dot(cm_ref[rows, :], zr) - _dot(sm_ref[rows, :], zi)) * dft_scale).astype(BF16)
            acc = _dot(ret_ref[n, rows, :], w_ref[0:RET_WIDTH, :])
            acc = acc + _dot(four, w_ref[RET_WIDTH:RET_WIDTH + FOURIER_WIDTH, :])
            acc = acc + _dot(conv_ref[n, rows, :], w_ref[RET_WIDTH + FOURIER_WIDTH:, :])
            y = x_ref[n, rows, :] + gate * acc
            o_ref[n, rows, :] = y
            h_ref[n, rows, :] = _mod_norm(y, shift, scale).astype(h_ref.dtype)


def _out_proj(x, mod, ret, conv, z, cos_m, sin_m, w_out, *, nb, tt):
    b, t, d = x.shape
    shared_mod = mod.shape[0] == 1
    assert nb == 1 or (shared_mod and tt == t)
    mod_map = (lambda j, i: (0, 0, 0)) if shared_mod else (lambda j, i: (i, 0, 0))
    blk = lambda w: pl.BlockSpec((nb, tt, w), lambda j, i: (i, j, 0))
    dft_blk = pl.BlockSpec((tt, t), lambda j, i: (j, 0), pipeline_mode=pl.Buffered(1))
    dft_scale = float(1.0 / np.sqrt(float(t) * FOURIER_GROUP_DIM))
    return pl.pallas_call(
        functools.partial(_out_proj_kernel, d_model=d, dft_scale=dft_scale),
        out_shape=(jax.ShapeDtypeStruct(x.shape, F32), jax.ShapeDtypeStruct(x.shape, BF16)),
        grid=(t // tt, b // nb),
        in_specs=[blk(d), pl.BlockSpec((1, 1, mod.shape[2]), mod_map),
                  blk(RET_WIDTH), blk(CONV_WIDTH),
                  pl.BlockSpec((nb, t, z.shape[2]), lambda j, i: (i, 0, 0)),
                  dft_blk, dft_blk, _resident(w_out.shape)],
        out_specs=(blk(d), blk(d)),
        compiler_params=_params(2),
        name="out_proj",
    )(x, mod, ret, conv, z, cos_m, sin_m, w_out)


def _ffn_kernel(*refs, d_model, hidden, fh, halo, final_norm):
    it = iter(refs)
    x_ref, h_ref = next(it), next(it)
    hp_ref = hn_ref = None
    if halo:
        hp_ref, hn_ref = next(it), next(it)
    mod_ref, wu_ref, cw_ref, wd_ref = (next(it) for _ in range(4))
    fg_ref = next(it) if final_norm else None
    o_ref, he_ref, act_ref = next(it), next(it), next(it)

    _, tt, d = x_ref.shape
    i = pl.program_id(1)
    he_ref[0:tt, :] = jnp.swapaxes(h_ref[0].reshape(HALO, tt // HALO, d), 0, 1).reshape(tt, d)
    if halo:
        rid = lax.broadcasted_iota(jnp.int32, (HALO, d), 0)
        before = jnp.where(i == 0, 0.0, hp_ref[0].astype(F32)[HALO - 1:HALO, :])
        after = jnp.where(i == pl.num_programs(1) - 1, 0.0, hn_ref[0].astype(F32)[0:1, :])
        edge = jnp.where(rid == 0, before, jnp.where(rid == 1, after, 0.0))
    else:
        edge = jnp.zeros((HALO, d), F32)
    he_ref[tt:tt + HALO, :] = edge.astype(BF16)

    def conv_rows(u, w):
        main = u[0:tt]
        first_prev = jnp.concatenate([u[tt:tt + 1], main[tt - HALO:tt - 1]], axis=0)
        last_next = jnp.concatenate([main[1:HALO], u[tt + 1:tt + 2]], axis=0)
        prev = jnp.concatenate([first_prev, main[0:tt - HALO]], axis=0)
        nxt = jnp.concatenate([main[HALO:tt], last_next], axis=0)
        return prev * w[0:1, :] + main * w[1:2, :] + nxt * w[2:3, :]

    for j in range(hidden // fh):
        vc = slice(j * fh, (j + 1) * fh)
        gc = slice(hidden + j * fh, hidden + (j + 1) * fh)
        he = he_ref[...]
        val = conv_rows(_dot(he, wu_ref[:, vc]), cw_ref[:, vc])
        gate = conv_rows(_dot(he, wu_ref[:, gc]), cw_ref[:, gc])
        act_ref[:, vc] = (val * (gate * jax.nn.sigmoid(gate))).astype(BF16)

    gate2 = mod_ref[0, :, 5 * d_model:6 * d_model]
    down = _dot(act_ref[...], wd_ref[...])
    y = x_ref[0] + gate2 * jnp.swapaxes(down.reshape(tt // HALO, HALO, d), 0, 1).reshape(tt, d)
    if final_norm:
        ms = jnp.mean(y * y, axis=-1, keepdims=True)
        y = (y * lax.rsqrt(ms + EPS)) * fg_ref[...]
    o_ref[0] = y


def _ffn(x, h, mod, w_up, conv_w, w_down, final_g, *, tt, fh):
    b, t, d = x.shape
    hidden = w_down.shape[0]
    assert hidden % fh == 0 and t % tt == 0 and tt % HALO == 0
    halo = tt < t
    shared_mod = mod.shape[0] == 1
    mod_map = (lambda bi, i: (0, 0, 0)) if shared_mod else (lambda bi, i: (bi, 0, 0))
    final_norm = final_g is not None
    tile = pl.BlockSpec((1, tt, d), lambda bi, i: (bi, i, 0))
    in_specs = [tile, tile]
    args = [x, h]
    if halo:
        per = tt // HALO
        last = t // HALO - 1
        in_specs += [
            pl.BlockSpec((1, HALO, d), lambda bi, i: (bi, jnp.maximum(i * per - 1, 0), 0)),
            pl.BlockSpec((1, HALO, d), lambda bi, i: (bi, jnp.minimum((i + 1) * per, last), 0)),
        ]
        args += [h, h]
    in_specs += [pl.BlockSpec((1, 1, mod.shape[2]), mod_map),
                 _resident(w_up.shape), _resident(conv_w.shape), _resident(w_down.shape)]
    args += [mod, w_up, conv_w, w_down]
    if final_norm:
        in_specs.append(_resident((1, d)))
        args.append(final_g.reshape(1, d))
    return pl.pallas_call(
        functools.partial(_ffn_kernel, d_model=d, hidden=hidden, fh=fh, halo=halo, final_norm=final_norm),
        out_shape=jax.ShapeDtypeStruct(x.shape, F32),
        grid=(b, t // tt),
        in_specs=in_specs,
        out_specs=tile,
        scratch_shapes=[pltpu.VMEM((tt + HALO, d), BF16), pltpu.VMEM((tt, hidden), BF16)],
        compiler_params=_params(2),
        name="ffn",
    )(*args)


def _rope_tables(t):
    rows = t // GRID_W
    row = jnp.repeat(jnp.arange(rows, dtype=F32), GRID_W)
    col = jnp.tile(jnp.arange(GRID_W, dtype=F32), rows)
    n_freq = HEAD_DIM // 4
    freq = ROPE_BASE ** (-jnp.arange(n_freq, dtype=F32) / n_freq)
    ang = jnp.concatenate([row[:, None] * freq, col[:, None] * freq], axis=-1)
    reps = LANES // ang.shape[1]
    cos = jnp.tile(jnp.cos(ang), (1, reps))
    sign = jnp.where((jnp.arange(LANES) % HEAD_DIM) < HEAD_DIM // 2, -1.0, 1.0).astype(F32)
    sin = jnp.tile(jnp.sin(ang), (1, reps)) * sign
    return cos, sin


def _decay_tables(decay_logit):
    ld = jax.nn.log_sigmoid(decay_logit.astype(F32))
    ld_f, ld_b = ld[0], ld[1]
    i = jnp.arange(CHUNK, dtype=F32)
    lane_f = jnp.repeat(ld_f, HEAD_DIM)[None, :]
    lane_b = jnp.repeat(ld_b, HEAD_DIM)[None, :]
    diff = i[:, None] - i[None, :]
    dmask = jnp.where(diff > 0, jnp.exp(ld_f[:, None, None] * jnp.maximum(diff, 0.0)),
                      jnp.where(diff < 0, jnp.exp(ld_b[:, None, None] * jnp.maximum(-diff, 0.0)), 2.0))
    dcat = dmask.reshape(HEAD_PAIRS, 2, CHUNK, CHUNK).transpose(0, 2, 1, 3).reshape(HEAD_PAIRS, CHUNK, 2 * CHUNK)
    return {
        "dcat": dcat,
        "xi_f": jnp.exp(lane_f * (i[:, None] + 1.0)),
        "xi_b": jnp.exp(lane_b * (CHUNK - i[:, None])),
        "zeta_f": jnp.exp(lane_f * (CHUNK - 1.0 - i[:, None])),
        "zeta_b": jnp.exp(lane_b * i[:, None]),
        "cd_f": jnp.exp(lane_f * CHUNK),
        "cd_b": jnp.exp(lane_b * CHUNK),
    }


def _position_dft(t):
    blk = GRID_W if t % GRID_W == 0 else 1
    r = jnp.arange(t, dtype=jnp.int32)[:, None]

    def table(cols):
        ang = ((r * cols[None, :]) % t).astype(F32) * (2.0 * np.pi / t)
        return jnp.cos(ang)[:, :, None], jnp.sin(ang)[:, :, None]

    ca, sa = table(jnp.arange(t // blk, dtype=jnp.int32) * blk)
    cb, sb = table(jnp.arange(blk, dtype=jnp.int32))
    cb, sb = cb.reshape(t, 1, blk), sb.reshape(t, 1, blk)
    cos = (ca * cb - sa * sb).reshape(t, t)
    sin = (sa * cb + ca * sb).reshape(t, t)
    return cos.astype(BF16), sin.astype(BF16)


def _channel_dft():
    n = FOURIER_GROUP_DIM
    idx = jnp.arange(n, dtype=jnp.int32)
    ang = ((idx[:, None] * idx[None, :]) % n).astype(F32) * (2.0 * np.pi / n)
    eye = jnp.eye(FOURIER_GROUPS, dtype=F32)
    return jnp.concatenate([jnp.kron(eye, jnp.cos(ang)), jnp.kron(eye, jnp.sin(ang))], axis=1).astype(BF16)


def kernel(x, c, ctx, c_ctx, w_mod, b_mod, w_in, ret_decay_logit, mix_conv_w, w_out,
           ffn_w_up, ffn_conv_w, ffn_w_down, final_norm_g):
    b, t, d = x.shape
    _, lc, _ = ctx.shape
    depth = w_mod.shape[0]

    rows = ((b + 1 + 7) // 8) * 8
    c_all = jnp.zeros((rows, d), F32).at[:b].set(c).at[b].set(c_ctx)
    mod_all = _modulation(c_all, w_mod, b_mod)

    rope = _rope_tables(t)
    dft_c = _channel_dft()
    dft_x = _position_dft(t)
    dft_ctx = _position_dft(lc)

    tt = min(t, 512)
    tt_big = min(t, 1024)
    nb_ctx = max(1, min(b, 1024 // lc))
    fh = 256

    for l in range(depth):
        last = l == depth - 1
        mod_x = mod_all[l, :b].reshape(b, 1, 6 * d)
        mod_c = mod_all[l, b:b + 1].reshape(1, 1, 6 * d)
        w_in_l = w_in[l].astype(BF16)
        w_out_l = w_out[l].astype(BF16)
        w_up_l = ffn_w_up[l].astype(BF16)
        w_down_l = ffn_w_down[l].astype(BF16)
        tables = _decay_tables(ret_decay_logit[l])

        if last:
            (kv,) = _in_proj(ctx, mod_c, w_in_l, tables, None, dft_c, nb=nb_ctx, tt=lc, kv_only=True)
            st_f, st_b = _retention_states(kv, tables)
        else:
            qkvg_c, kz_c, z_c, pc_c = _in_proj(ctx, mod_c, w_in_l, tables, None, dft_c, nb=nb_ctx, tt=lc)
            ret_c, conv_c, st_f, st_b = _mixers(qkvg_c, kz_c, pc_c, tables, mix_conv_w[l], None,
                                                out_states=True)

        qkvg, kz, z, pc = _in_proj(x, mod_x, w_in_l, tables, rope, dft_c, nb=1, tt=tt_big)
        ret, conv = _mixers(qkvg, kz, pc, tables, mix_conv_w[l], (st_f, st_b), out_states=False)
        x, h = _out_proj(x, mod_x, ret, conv, z, *dft_x, w_out_l, nb=1, tt=tt_big)
        x = _ffn(x, h, mod_x, w_up_l, ffn_conv_w[l], w_down_l, final_norm_g if last else None, tt=tt_big, fh=fh)

        if not last:
            ctx, h_c = _out_proj(ctx, mod_c, ret_c, conv_c, z_c, *dft_ctx, w_out_l, nb=nb_ctx, tt=lc)
            ctx = _ffn(ctx, h_c, mod_c, w_up_l, ffn_conv_w[l], w_down_l, None, tt=lc, fh=fh)
    return x
```

```python
import functools

import jax
import jax.numpy as jnp
import numpy as np
from jax import lax
from jax.experimental import pallas as pl
from jax.experimental.pallas import tpu as pltpu

F32 = jnp.float32
BF16 = jnp.bfloat16

GRID_W = 64
RET_HEADS = 8
HEAD_DIM = 64
RET_WIDTH = RET_HEADS * HEAD_DIM
FOURIER_GROUPS = 4
FOURIER_GROUP_DIM = 64
FOURIER_WIDTH = FOURIER_GROUPS * FOURIER_GROUP_DIM
CONV_WIDTH = 256
ROPE_BASE = 10000.0
EPS = 1e-6
CHUNK = 128

LANES = 128
BF16_SUBLANES = 16
V7X_VMEM_LIMIT_BYTES = 60000 * 1024

HEAD_PAIRS = RET_WIDTH // LANES
HALO = BF16_SUBLANES
OUT_PROJ_ROWS = 512

Q0 = 0
K0 = Q0 + RET_WIDTH
V0 = K0 + RET_WIDTH
G0 = V0 + RET_WIDTH
F0 = G0 + RET_WIDTH
CB0 = F0 + FOURIER_WIDTH
IN_WIDTH = CB0 + 3 * CONV_WIDTH


def _params(n_grid_dims, vmem=V7X_VMEM_LIMIT_BYTES):
    return pltpu.CompilerParams(
        dimension_semantics=("arbitrary",) * n_grid_dims,
        vmem_limit_bytes=vmem,
    )


def _dot(a, b):
    return jnp.dot(a, b, preferred_element_type=F32)


def _mod_norm(x, shift, scale):
    ms = jnp.mean(x * x, axis=-1, keepdims=True)
    return (x * lax.rsqrt(ms + EPS)) * (1.0 + scale) + shift


def _resident(shape):
    return pl.BlockSpec(shape, lambda *_: (0,) * len(shape), pipeline_mode=pl.Buffered(1))


def _mod_kernel(c_ref, w_ref, b_ref, o_ref):
    c = c_ref[...]
    s = c * jax.nn.sigmoid(c)
    o_ref[0] = jnp.dot(s, w_ref[0], preferred_element_type=F32,
                       precision=lax.Precision.HIGHEST) + b_ref[0]


def _modulation(c_all, w_mod, b_mod):
    depth, d, n = w_mod.shape
    rows = c_all.shape[0]
    tn = 1024
    return pl.pallas_call(
        _mod_kernel,
        out_shape=jax.ShapeDtypeStruct((depth, rows, n), F32),
        grid=(depth, n // tn),
        in_specs=[
            pl.BlockSpec((rows, d), lambda l, j: (0, 0)),
            pl.BlockSpec((1, d, tn), lambda l, j: (l, 0, j)),
            pl.BlockSpec((1, 1, tn), lambda l, j: (l, 0, j)),
        ],
        out_specs=pl.BlockSpec((1, rows, tn), lambda l, j: (l, 0, j)),
        compiler_params=_params(2),
        name="modulation",
    )(c_all, w_mod, b_mod.reshape(depth, 1, n))


def _rope_block(t, cos, sin, first_half):
    lo = pltpu.roll(t, LANES - HEAD_DIM // 2, 1)
    hi = pltpu.roll(t, HEAD_DIM // 2, 1)
    return t * cos + jnp.where(first_half, lo, hi) * sin


def _in_proj_kernel(*refs, d_model, use_rope, kv_only):
    it = iter(refs)
    x_ref, mod_ref, w_ref, zf_ref, zb_ref = (next(it) for _ in range(5))
    cos_ref = sin_ref = dft_ref = None
    if use_rope:
        cos_ref, sin_ref = next(it), next(it)
    if not kv_only:
        dft_ref = next(it)
    outs = list(it)

    nb, tt, d = x_ref.shape
    m = nb * tt
    x = x_ref[...].reshape(m, d)
    shift = mod_ref[0, :, 0:d_model]
    scale = mod_ref[0, :, d_model:2 * d_model]
    h = _mod_norm(x, shift, scale).astype(BF16)

    def put(ref, col, val):
        ref[:, :, col:col + val.shape[1]] = val.astype(ref.dtype).reshape(nb, tt, val.shape[1])

    if use_rope:
        cos = cos_ref[...]
        sin = sin_ref[...]
        lane = lax.broadcasted_iota(jnp.int32, (m, LANES), 1)
        first_half = (lane % HEAD_DIM) < (HEAD_DIM // 2)

    def rotary(t, j):
        tb = t[:, j * LANES:(j + 1) * LANES]
        return _rope_block(tb, cos, sin, first_half) if use_rope else tb

    def chunk_tiled(ref, j):
        tab = ref[:, j * LANES:(j + 1) * LANES]
        return jnp.concatenate([tab] * (m // CHUNK), axis=0)

    kv_ref = outs[0] if kv_only else outs[4]
    k = _dot(h, w_ref[:, K0:V0])
    for j in range(HEAD_PAIRS):
        kb = rotary(k, j) * (HEAD_DIM ** -0.5)
        if not kv_only:
            put(outs[1], j * LANES, kb)
        put(kv_ref, j * LANES, kb * chunk_tiled(zf_ref, j))
        put(kv_ref, RET_WIDTH + j * LANES, kb * chunk_tiled(zb_ref, j))
    v = _dot(h, w_ref[:, V0:G0])
    if kv_only:
        put(kv_ref, 2 * RET_WIDTH, v)
        return

    q_ref, _, v_ref, g_ref, _, z_ref, pc_ref = outs
    put(v_ref, 0, v)
    q = _dot(h, w_ref[:, Q0:K0])
    for j in range(HEAD_PAIRS):
        put(q_ref, j * LANES, rotary(q, j))
    g = _dot(h, w_ref[:, G0:F0])
    put(g_ref, 0, g * jax.nn.sigmoid(g))
    fpc = _dot(h, w_ref[:, F0:IN_WIDTH])
    f = fpc[:, 0:FOURIER_WIDTH].astype(BF16)
    put(z_ref, 0, _dot(f, dft_ref[...]))
    put(pc_ref, 0, fpc[:, CB0 - F0:CB0 - F0 + CONV_WIDTH])
    put(pc_ref, CONV_WIDTH, fpc[:, CB0 - F0 + CONV_WIDTH:CB0 - F0 + 2 * CONV_WIDTH]
        * fpc[:, CB0 - F0 + 2 * CONV_WIDTH:])


def _in_proj(x, mod, w_in, tables, rope, dft_c, *, nb, tt, kv_only=False):
    b, t, d = x.shape
    use_rope = rope is not None
    shared_mod = mod.shape[0] == 1
    assert nb == 1 or (shared_mod and not use_rope)
    assert tt % CHUNK == 0
    mod_map = (lambda i, j: (0, 0, 0)) if shared_mod else (lambda i, j: (i, 0, 0))
    in_specs = [
        pl.BlockSpec((nb, tt, d), lambda i, j: (i, j, 0)),
        pl.BlockSpec((1, 1, mod.shape[2]), mod_map),
        _resident(w_in.shape),
        _resident(tables["zeta_f"].shape),
        _resident(tables["zeta_b"].shape),
    ]
    args = [x, mod, w_in, tables["zeta_f"], tables["zeta_b"]]
    if use_rope:
        in_specs += [pl.BlockSpec((tt, LANES), lambda i, j: (j, 0))] * 2
        args += list(rope)
    if kv_only:
        widths = (3 * RET_WIDTH,)
    else:
        in_specs.append(_resident(dft_c.shape))
        args.append(dft_c)
        widths = (RET_WIDTH,) * 4 + (2 * RET_WIDTH, 2 * FOURIER_WIDTH, 2 * CONV_WIDTH)
    out_shape = tuple(jax.ShapeDtypeStruct((b, t, w), BF16) for w in widths)
    out_specs = tuple(pl.BlockSpec((nb, tt, w), lambda i, j: (i, j, 0)) for w in widths)
    return pl.pallas_call(
        functools.partial(_in_proj_kernel, d_model=d, use_rope=use_rope, kv_only=kv_only),
        out_shape=out_shape,
        grid=(b // nb, t // tt),
        in_specs=in_specs,
        out_specs=out_specs,
        compiler_params=_params(2),
        name="in_proj_kv" if kv_only else "in_proj",
    )(*args)


def _block_diag_mask():
    row = lax.broadcasted_iota(jnp.int32, (LANES, LANES), 0)
    col = lax.broadcasted_iota(jnp.int32, (LANES, LANES), 1)
    return (row < HEAD_DIM) == (col < HEAD_DIM)


def _state_sweep(kz_ref, v_ref, v_col, cdf_ref, cdb_ref, sf_ref, sb_ref, s_all, n_chunks):
    diag = _block_diag_mask()

    def update(state_ref, p, kz, v, decay):
        u = lax.dot_general(kz, v, (((0,), (0,)), ((), ())), preferred_element_type=F32)
        state_ref[p] = decay * state_ref[p] + jnp.where(diag, u, 0.0)

    def body(i, carry):
        rf = pl.multiple_of(i * CHUNK, CHUNK)
        nb_ = n_chunks - 1 - i
        rb = pl.multiple_of(nb_ * CHUNK, CHUNK)
        for p in range(HEAD_PAIRS):
            lanes = slice(p * LANES, (p + 1) * LANES)
            vl = slice(v_col + p * LANES, v_col + (p + 1) * LANES)
            if s_all is not None:
                s_all[i, p, :, 0:LANES] = sf_ref[p].astype(BF16)
                s_all[nb_, p, :, LANES:2 * LANES] = sb_ref[p].astype(BF16)
            update(sf_ref, p, kz_ref[0, pl.ds(rf, CHUNK), lanes], v_ref[0, pl.ds(rf, CHUNK), vl],
                   cdf_ref[:, lanes])
            update(sb_ref, p, kz_ref[0, pl.ds(rb, CHUNK), RET_WIDTH + p * LANES:RET_WIDTH + (p + 1) * LANES],
                   v_ref[0, pl.ds(rb, CHUNK), vl], cdb_ref[:, lanes])
        return carry

    lax.fori_loop(0, n_chunks, body, 0, unroll=8 if n_chunks % 8 == 0 else (2 if n_chunks % 2 == 0 else 1))


def _retention_states_kernel(kv_ref, cdf_ref, cdb_ref, sf_out, sb_out, *, t):
    sf_out[...] = jnp.zeros(sf_out.shape, F32)
    sb_out[...] = jnp.zeros(sb_out.shape, F32)
    _state_sweep(kv_ref, kv_ref, 2 * RET_WIDTH, cdf_ref, cdb_ref, sf_out.at[0], sb_out.at[0], None,
                 t // CHUNK)


def _retention_states(kv, tables):
    b, t, _ = kv.shape
    state = jax.ShapeDtypeStruct((b, HEAD_PAIRS, LANES, LANES), F32)
    ins = [tables["cd_f"], tables["cd_b"]]
    return pl.pallas_call(
        functools.partial(_retention_states_kernel, t=t),
        out_shape=(state, state),
        grid=(b,),
        in_specs=[pl.BlockSpec((1, t, kv.shape[2]), lambda i: (i, 0, 0))] + [_resident(a.shape) for a in ins],
        out_specs=(pl.BlockSpec((1, HEAD_PAIRS, LANES, LANES), lambda i: (i, 0, 0, 0)),) * 2,
        compiler_params=_params(1),
        name="retention_states",
    )(kv, *ins)


def _mixers_kernel(*refs, t, has_init, out_states):
    it = iter(refs)
    q_ref, k_ref, v_ref, g_ref, kz_ref, pc_ref = (next(it) for _ in range(6))
    dcat_ref, xif_ref, xib_ref, cdf_ref, cdb_ref, cw_ref = (next(it) for _ in range(6))
    if has_init:
        if_ref, ib_ref = next(it), next(it)
    ret_ref, conv_ref = next(it), next(it)
    if out_states:
        sf_out, sb_out = next(it), next(it)
    sf_ref, sb_ref, s_all = next(it), next(it), next(it)

    n_chunks = t // CHUNK
    if has_init:
        sf_ref[...] = if_ref[0]
        sb_ref[...] = ib_ref[0]
    else:
        sf_ref[...] = jnp.zeros(sf_ref.shape, F32)
        sb_ref[...] = jnp.zeros(sb_ref.shape, F32)

    _state_sweep(kz_ref, v_ref, 0, cdf_ref, cdb_ref, sf_ref, sb_ref, s_all, n_chunks)
    if out_states:
        sf_out[0] = sf_ref[...]
        sb_out[0] = sb_ref[...]

    lane = lax.broadcasted_iota(jnp.int32, (1, LANES), 1)
    head0 = lane < HEAD_DIM
    m0 = jnp.where(head0, 1.0, 0.0).astype(BF16)
    m1 = jnp.where(head0, 0.0, 1.0).astype(BF16)
    w = cw_ref[...]
    a0 = jnp.where(head0, 1.0 / HEAD_DIM, 0.0)
    a1 = jnp.where(head0, 0.0, 1.0 / HEAD_DIM)

    def out_chunk(n, carry):
        r0 = pl.multiple_of(n * CHUNK, CHUNK)
        rows = pl.ds(r0, CHUNK)
        for p in range(HEAD_PAIRS):
            lanes = slice(p * LANES, (p + 1) * LANES)
            q = q_ref[0, rows, lanes]
            k = k_ref[0, rows, lanes]
            v = v_ref[0, rows, lanes]
            kk = jnp.concatenate([k * m0, k * m1], axis=0)
            vv = jnp.concatenate([v * m0, v * m1], axis=0)
            s = lax.dot_general(q, kk, (((1,), (1,)), ((), ())), preferred_element_type=F32)
            pr = (s * dcat_ref[p]).astype(BF16)
            o = _dot(pr, vv)
            c = _dot(q, s_all[n, p])
            o = o + c[:, 0:LANES] * xif_ref[:, lanes] + c[:, LANES:2 * LANES] * xib_ref[:, lanes]
            o2 = o * o + EPS
            r0n = lax.rsqrt(jnp.sum(o2 * a0, axis=-1, keepdims=True))
            r1n = lax.rsqrt(jnp.sum(o2 * a1, axis=-1, keepdims=True))
            o = o * jnp.where(head0, r0n, r1n)
            ret_ref[0, rows, lanes] = (o * g_ref[0, rows, lanes].astype(F32)).astype(ret_ref.dtype)

        def gate_prod(start, size):
            return pc_ref[0, pl.ds(start, size), CONV_WIDTH:2 * CONV_WIDTH].astype(F32)

        lo = pl.multiple_of(jnp.maximum(r0 - HALO, 0), HALO)
        hi = pl.multiple_of(jnp.minimum(r0 + CHUNK, t - HALO), HALO)
        before = jnp.where(n == 0, 0.0, gate_prod(lo, HALO))
        after = jnp.where(n == n_chunks - 1, 0.0, gate_prod(hi, HALO))
        mid = gate_prod(r0, CHUNK)
        ext = jnp.concatenate([before, mid, after], axis=0)
        acc = (ext[HALO - 1:HALO - 1 + CHUNK] * w[0:1, :] + mid * w[1:2, :]
               + ext[HALO + 1:HALO + 1 + CHUNK] * w[2:3, :])
        cb = pc_ref[0, rows, 0:CONV_WIDTH].astype(F32)
        conv_ref[0, rows, :] = (cb * acc).astype(conv_ref.dtype)
        return carry

    lax.fori_loop(0, n_chunks, out_chunk, 0, unroll=8 if n_chunks % 8 == 0 else 2)


def _mixers(q, k, v, g, kz, pc, tables, conv_w, init, *, out_states):
    b, t, _ = q.shape
    has_init = init is not None
    n_chunks = t // CHUNK
    seq = lambda a: pl.BlockSpec((1, t, a.shape[2]), lambda i: (i, 0, 0))
    tabs = [tables[k] for k in ("dcat", "xi_f", "xi_b", "cd_f", "cd_b")] + [conv_w]
    args = [q, k, v, g, kz, pc]
    in_specs = [seq(a) for a in args] + [_resident(a.shape) for a in tabs]
    args = args + tabs
    state_spec = pl.BlockSpec((1, HEAD_PAIRS, LANES, LANES), lambda i: (i, 0, 0, 0))
    if has_init:
        in_specs += [state_spec, state_spec]
        args += list(init)
    out_shape = [jax.ShapeDtypeStruct((b, t, RET_WIDTH), BF16),
                 jax.ShapeDtypeStruct((b, t, CONV_WIDTH), BF16)]
    out_specs = [pl.BlockSpec((1, t, RET_WIDTH), lambda i: (i, 0, 0)),
                 pl.BlockSpec((1, t, CONV_WIDTH), lambda i: (i, 0, 0))]
    if out_states:
        out_shape += [jax.ShapeDtypeStruct((b, HEAD_PAIRS, LANES, LANES), F32)] * 2
        out_specs += [state_spec, state_spec]
    return pl.pallas_call(
        functools.partial(_mixers_kernel, t=t, has_init=has_init, out_states=out_states),
        out_shape=tuple(out_shape),
        grid=(b,),
        in_specs=in_specs,
        out_specs=tuple(out_specs),
        scratch_shapes=[
            pltpu.VMEM((HEAD_PAIRS, LANES, LANES), F32),
            pltpu.VMEM((HEAD_PAIRS, LANES, LANES), F32),
            pltpu.VMEM((n_chunks, HEAD_PAIRS, LANES, 2 * LANES), BF16),
        ],
        compiler_params=_params(1),
        name="mixers",
    )(*args)


def _out_proj_kernel(x_ref, mod_ref, ret_ref, conv_ref, z_ref, cm_ref, sm_ref, w_ref, o_ref, h_ref,
                     *, d_model, dft_scale):
    nb, tt, d = x_ref.shape
    gate = mod_ref[0, :, 2 * d_model:3 * d_model]
    shift = mod_ref[0, :, 3 * d_model:4 * d_model]
    scale = mod_ref[0, :, 4 * d_model:5 * d_model]
    rb = min(tt, OUT_PROJ_ROWS)
    for n in range(nb):
        zr = z_ref[n, :, 0:FOURIER_WIDTH]
        zi = z_ref[n, :, FOURIER_WIDTH:2 * FOURIER_WIDTH]
        for r in range(tt // rb):
            rows = slice(r * rb, (r + 1) * rb)
            four = ((_dot(cm_ref[rows, :], zr) - _dot(sm_ref[rows, :], zi)) * dft_scale).astype(BF16)
            acc = _dot(ret_ref[n, rows, :], w_ref[0:RET_WIDTH, :])
            acc = acc + _dot(four, w_ref[RET_WIDTH:RET_WIDTH + FOURIER_WIDTH, :])
            acc = acc + _dot(conv_ref[n, rows, :], w_ref[RET_WIDTH + FOURIER_WIDTH:, :])
            y = x_ref[n, rows, :] + gate * acc
            o_ref[n, rows, :] = y
            h_ref[n, rows, :] = _mod_norm(y, shift, scale).astype(h_ref.dtype)


def _out_proj(x, mod, ret, conv, z, cos_m, sin_m, w_out, *, nb, tt):
    b, t, d = x.shape
    shared_mod = mod.shape[0] == 1
    assert nb == 1 or (shared_mod and tt == t)
    mod_map = (lambda j, i: (0, 0, 0)) if shared_mod else (lambda j, i: (i, 0, 0))
    blk = lambda w: pl.BlockSpec((nb, tt, w), lambda j, i: (i, j, 0))
    dft_blk = pl.BlockSpec((tt, t), lambda j, i: (j, 0), pipeline_mode=pl.Buffered(1))
    dft_scale = float(1.0 / np.sqrt(float(t) * FOURIER_GROUP_DIM))
    return pl.pallas_call(
        functools.partial(_out_proj_kernel, d_model=d, dft_scale=dft_scale),
        out_shape=(jax.ShapeDtypeStruct(x.shape, F32), jax.ShapeDtypeStruct(x.shape, BF16)),
        grid=(t // tt, b // nb),
        in_specs=[blk(d), pl.BlockSpec((1, 1, mod.shape[2]), mod_map),
                  blk(RET_WIDTH), blk(CONV_WIDTH),
                  pl.BlockSpec((nb, t, z.shape[2]), lambda j, i: (i, 0, 0)),
                  dft_blk, dft_blk, _resident(w_out.shape)],
        out_specs=(blk(d), blk(d)),
        compiler_params=_params(2),
        name="out_proj",
    )(x, mod, ret, conv, z, cos_m, sin_m, w_out)


def _ffn_kernel(*refs, d_model, hidden, fh, halo, final_norm):
    it = iter(refs)
    x_ref, h_ref = next(it), next(it)
    hp_ref = hn_ref = None
    if halo:
        hp_ref, hn_ref = next(it), next(it)
    mod_ref, wu_ref, cw_ref, wd_ref = (next(it) for _ in range(4))
    fg_ref = next(it) if final_norm else None
    o_ref, he_ref, act_ref = next(it), next(it), next(it)

    _, tt, d = x_ref.shape
    i = pl.program_id(1)
    he_ref[0:tt, :] = jnp.swapaxes(h_ref[0].reshape(HALO, tt // HALO, d), 0, 1).reshape(tt, d)
    if halo:
        rid = lax.broadcasted_iota(jnp.int32, (HALO, d), 0)
        before = jnp.where(i == 0, 0.0, hp_ref[0].astype(F32)[HALO - 1:HALO, :])
        after = jnp.where(i == pl.num_programs(1) - 1, 0.0, hn_ref[0].astype(F32)[0:1, :])
        edge = jnp.where(rid == 0, before, jnp.where(rid == 1, after, 0.0))
    else:
        edge = jnp.zeros((HALO, d), F32)
    he_ref[tt:tt + HALO, :] = edge.astype(BF16)

    def conv_rows(u, w):
        main = u[0:tt]
        first_prev = jnp.concatenate([u[tt:tt + 1], main[tt - HALO:tt - 1]], axis=0)
        last_next = jnp.concatenate([main[1:HALO], u[tt + 1:tt + 2]], axis=0)
        prev = jnp.concatenate([first_prev, main[0:tt - HALO]], axis=0)
        nxt = jnp.concatenate([main[HALO:tt], last_next], axis=0)
        return prev * w[0:1, :] + main * w[1:2, :] + nxt * w[2:3, :]

    for j in range(hidden // fh):
        vc = slice(j * fh, (j + 1) * fh)
        gc = slice(hidden + j * fh, hidden + (j + 1) * fh)
        he = he_ref[...]
        val = conv_rows(_dot(he, wu_ref[:, vc]), cw_ref[:, vc])
        gate = conv_rows(_dot(he, wu_ref[:, gc]), cw_ref[:, gc])
        act_ref[:, vc] = (val * (gate * jax.nn.sigmoid(gate))).astype(BF16)

    gate2 = mod_ref[0, :, 5 * d_model:6 * d_model]
    down = _dot(act_ref[...], wd_ref[...])
    y = x_ref[0] + gate2 * jnp.swapaxes(down.reshape(tt // HALO, HALO, d), 0, 1).reshape(tt, d)
    if final_norm:
        ms = jnp.mean(y * y, axis=-1, keepdims=True)
        y = (y * lax.rsqrt(ms + EPS)) * fg_ref[...]
    o_ref[0] = y


def _ffn(x, h, mod, w_up, conv_w, w_down, final_g, *, tt, fh):
    b, t, d = x.shape
    hidden = w_down.shape[0]
    assert hidden % fh == 0 and t % tt == 0 and tt % HALO == 0
    halo = tt < t
    shared_mod = mod.shape[0] == 1
    mod_map = (lambda bi, i: (0, 0, 0)) if shared_mod else (lambda bi, i: (bi, 0, 0))
    final_norm = final_g is not None
    tile = pl.BlockSpec((1, tt, d), lambda bi, i: (bi, i, 0))
    in_specs = [tile, tile]
    args = [x, h]
    if halo:
        per = tt // HALO
        last = t // HALO - 1
        in_specs += [
            pl.BlockSpec((1, HALO, d), lambda bi, i: (bi, jnp.maximum(i * per - 1, 0), 0)),
            pl.BlockSpec((1, HALO, d), lambda bi, i: (bi, jnp.minimum((i + 1) * per, last), 0)),
        ]
        args += [h, h]
    in_specs += [pl.BlockSpec((1, 1, mod.shape[2]), mod_map),
                 _resident(w_up.shape), _resident(conv_w.shape), _resident(w_down.shape)]
    args += [mod, w_up, conv_w, w_down]
    if final_norm:
        in_specs.append(_resident((1, d)))
        args.append(final_g.reshape(1, d))
    return pl.pallas_call(
        functools.partial(_ffn_kernel, d_model=d, hidden=hidden, fh=fh, halo=halo, final_norm=final_norm),
        out_shape=jax.ShapeDtypeStruct(x.shape, F32),
        grid=(b, t // tt),
        in_specs=in_specs,
        out_specs=tile,
        scratch_shapes=[pltpu.VMEM((tt + HALO, d), BF16), pltpu.VMEM((tt, hidden), BF16)],
        compiler_params=_params(2),
        name="ffn",
    )(*args)


def _rope_tables(t):
    rows = t // GRID_W
    row = jnp.repeat(jnp.arange(rows, dtype=F32), GRID_W)
    col = jnp.tile(jnp.arange(GRID_W, dtype=F32), rows)
    n_freq = HEAD_DIM // 4
    freq = ROPE_BASE ** (-jnp.arange(n_freq, dtype=F32) / n_freq)
    ang = jnp.concatenate([row[:, None] * freq, col[:, None] * freq], axis=-1)
    reps = LANES // ang.shape[1]
    cos = jnp.tile(jnp.cos(ang), (1, reps))
    sign = jnp.where((jnp.arange(LANES) % HEAD_DIM) < HEAD_DIM // 2, -1.0, 1.0).astype(F32)
    sin = jnp.tile(jnp.sin(ang), (1, reps)) * sign
    return cos, sin


def _decay_tables(decay_logit):
    ld = jax.nn.log_sigmoid(decay_logit.astype(F32))
    ld_f, ld_b = ld[0], ld[1]
    i = jnp.arange(CHUNK, dtype=F32)
    lane_f = jnp.repeat(ld_f, HEAD_DIM)[None, :]
    lane_b = jnp.repeat(ld_b, HEAD_DIM)[None, :]
    diff = i[:, None] - i[None, :]
    dmask = jnp.where(diff > 0, jnp.exp(ld_f[:, None, None] * jnp.maximum(diff, 0.0)),
                      jnp.where(diff < 0, jnp.exp(ld_b[:, None, None] * jnp.maximum(-diff, 0.0)), 2.0))
    dcat = dmask.reshape(HEAD_PAIRS, 2, CHUNK, CHUNK).transpose(0, 2, 1, 3).reshape(HEAD_PAIRS, CHUNK, 2 * CHUNK)
    return {
        "dcat": dcat,
        "xi_f": jnp.exp(lane_f * (i[:, None] + 1.0)),
        "xi_b": jnp.exp(lane_b * (CHUNK - i[:, None])),
        "zeta_f": jnp.exp(lane_f * (CHUNK - 1.0 - i[:, None])),
        "zeta_b": jnp.exp(lane_b * i[:, None]),
        "cd_f": jnp.exp(lane_f * CHUNK),
        "cd_b": jnp.exp(lane_b * CHUNK),
    }


def _position_dft(t):
    blk = GRID_W if t % GRID_W == 0 else 1
    r = jnp.arange(t, dtype=jnp.int32)[:, None]

    def table(cols):
        ang = ((r * cols[None, :]) % t).astype(F32) * (2.0 * np.pi / t)
        return jnp.cos(ang)[:, :, None], jnp.sin(ang)[:, :, None]

    ca, sa = table(jnp.arange(t // blk, dtype=jnp.int32) * blk)
    cb, sb = table(jnp.arange(blk, dtype=jnp.int32))
    cb, sb = cb.reshape(t, 1, blk), sb.reshape(t, 1, blk)
    cos = (ca * cb - sa * sb).reshape(t, t)
    sin = (sa * cb + ca * sb).reshape(t, t)
    return cos.astype(BF16), sin.astype(BF16)


def _channel_dft():
    n = FOURIER_GROUP_DIM
    idx = jnp.arange(n, dtype=jnp.int32)
    ang = ((idx[:, None] * idx[None, :]) % n).astype(F32) * (2.0 * np.pi / n)
    eye = jnp.eye(FOURIER_GROUPS, dtype=F32)
    return jnp.concatenate([jnp.kron(eye, jnp.cos(ang)), jnp.kron(eye, jnp.sin(ang))], axis=1).astype(BF16)


def kernel(x, c, ctx, c_ctx, w_mod, b_mod, w_in, ret_decay_logit, mix_conv_w, w_out,
           ffn_w_up, ffn_conv_w, ffn_w_down, final_norm_g):
    b, t, d = x.shape
    _, lc, _ = ctx.shape
    depth = w_mod.shape[0]

    rows = ((b + 1 + 7) // 8) * 8
    c_all = jnp.zeros((rows, d), F32).at[:b].set(c).at[b].set(c_ctx)
    mod_all = _modulation(c_all, w_mod, b_mod)

    rope = _rope_tables(t)
    dft_c = _channel_dft()
    dft_x = _position_dft(t)
    dft_ctx = _position_dft(lc)

    tt = min(t, 512)
    tt_big = min(t, 1024)
    nb_ctx = max(1, min(b, 1024 // lc))
    fh = 256

    for l in range(depth):
        last = l == depth - 1
        mod_x = mod_all[l, :b].reshape(b, 1, 6 * d)
        mod_c = mod_all[l, b:b + 1].reshape(1, 1, 6 * d)
        w_in_l = w_in[l].astype(BF16)
        w_out_l = w_out[l].astype(BF16)
        w_up_l = ffn_w_up[l].astype(BF16)
        w_down_l = ffn_w_down[l].astype(BF16)
        tables = _decay_tables(ret_decay_logit[l])

        if last:
            (kv,) = _in_proj(ctx, mod_c, w_in_l, tables, None, dft_c, nb=nb_ctx, tt=lc, kv_only=True)
            st_f, st_b = _retention_states(kv, tables)
        else:
            *qkvgz_c, z_c, pc_c = _in_proj(ctx, mod_c, w_in_l, tables, None, dft_c, nb=nb_ctx, tt=lc)
            ret_c, conv_c, st_f, st_b = _mixers(*qkvgz_c, pc_c, tables, mix_conv_w[l], None, out_states=True)

        *qkvgz, z, pc = _in_proj(x, mod_x, w_in_l, tables, rope, dft_c, nb=1, tt=tt_big)
        ret, conv = _mixers(*qkvgz, pc, tables, mix_conv_w[l], (st_f, st_b), out_states=False)
        x, h = _out_proj(x, mod_x, ret, conv, z, *dft_x, w_out_l, nb=1, tt=tt_big)
        x = _ffn(x, h, mod_x, w_up_l, ffn_conv_w[l], w_down_l, final_norm_g if last else None, tt=tt_big, fh=fh)

        if not last:
            ctx, h_c = _out_proj(ctx, mod_c, ret_c, conv_c, z_c, *dft_ctx, w_out_l, nb=nb_ctx, tt=lc)
            ctx = _ffn(ctx, h_c, mod_c, w_up_l, ffn_conv_w[l], w_down_l, None, tt=lc, fh=fh)
    return x
```

```python
import functools

import jax
import jax.numpy as jnp
import numpy as np
from jax import lax
from jax.experimental import pallas as pl
from jax.experimental.pallas import tpu as pltpu

F32 = jnp.float32
BF16 = jnp.bfloat16

GRID_W = 64
RET_HEADS = 8
HEAD_DIM = 64
RET_WIDTH = RET_HEADS * HEAD_DIM
FOURIER_GROUPS = 4
FOURIER_GROUP_DIM = 64
FOURIER_WIDTH = FOURIER_GROUPS * FOURIER_GROUP_DIM
CONV_WIDTH = 256
ROPE_BASE = 10000.0
EPS = 1e-6
CHUNK = 128

LANES = 128
BF16_SUBLANES = 16
V7X_VMEM_LIMIT_BYTES = 60000 * 1024

HEAD_PAIRS = RET_WIDTH // LANES
HALO = BF16_SUBLANES
OUT_PROJ_ROWS = 512

Q0 = 0
K0 = Q0 + RET_WIDTH
V0 = K0 + RET_WIDTH
G0 = V0 + RET_WIDTH
F0 = G0 + RET_WIDTH
CB0 = F0 + FOURIER_WIDTH
IN_WIDTH = CB0 + 3 * CONV_WIDTH


def _params(n_grid_dims, vmem=V7X_VMEM_LIMIT_BYTES):
    return pltpu.CompilerParams(
        dimension_semantics=("arbitrary",) * n_grid_dims,
        vmem_limit_bytes=vmem,
    )


def _dot(a, b):
    return jnp.dot(a, b, preferred_element_type=F32)


def _mod_norm(x, shift, scale):
    ms = jnp.mean(x * x, axis=-1, keepdims=True)
    return (x * lax.rsqrt(ms + EPS)) * (1.0 + scale) + shift


def _resident(shape):
    return pl.BlockSpec(shape, lambda *_: (0,) * len(shape), pipeline_mode=pl.Buffered(1))


def _mod_kernel(c_ref, w_ref, b_ref, o_ref):
    c = c_ref[...]
    s = c * jax.nn.sigmoid(c)
    o_ref[0] = jnp.dot(s, w_ref[0], preferred_element_type=F32,
                       precision=lax.Precision.HIGHEST) + b_ref[0]


def _modulation(c_all, w_mod, b_mod):
    depth, d, n = w_mod.shape
    rows = c_all.shape[0]
    tn = 1024
    return pl.pallas_call(
        _mod_kernel,
        out_shape=jax.ShapeDtypeStruct((depth, rows, n), F32),
        grid=(depth, n // tn),
        in_specs=[
            pl.BlockSpec((rows, d), lambda l, j: (0, 0)),
            pl.BlockSpec((1, d, tn), lambda l, j: (l, 0, j)),
            pl.BlockSpec((1, 1, tn), lambda l, j: (l, 0, j)),
        ],
        out_specs=pl.BlockSpec((1, rows, tn), lambda l, j: (l, 0, j)),
        compiler_params=_params(2),
        name="modulation",
    )(c_all, w_mod, b_mod.reshape(depth, 1, n))


def _rope_block(t, cos, sin, first_half):
    lo = pltpu.roll(t, LANES - HEAD_DIM // 2, 1)
    hi = pltpu.roll(t, HEAD_DIM // 2, 1)
    return t * cos + jnp.where(first_half, lo, hi) * sin


def _in_proj_kernel(*refs, d_model, use_rope, kv_only):
    it = iter(refs)
    x_ref, mod_ref, w_ref, zf_ref, zb_ref = (next(it) for _ in range(5))
    cos_ref = sin_ref = dft_ref = None
    if use_rope:
        cos_ref, sin_ref = next(it), next(it)
    if not kv_only:
        dft_ref = next(it)
    outs = list(it)

    nb, tt, d = x_ref.shape
    m = nb * tt
    x = x_ref[...].reshape(m, d)
    shift = mod_ref[0, :, 0:d_model]
    scale = mod_ref[0, :, d_model:2 * d_model]
    h = _mod_norm(x, shift, scale).astype(BF16)

    def put(ref, col, val):
        ref[:, :, col:col + val.shape[1]] = val.astype(ref.dtype).reshape(nb, tt, val.shape[1])

    def put_block(ref, j, val):
        ref[:, j, :, :] = val.astype(ref.dtype).reshape(nb, tt, LANES)

    if use_rope:
        cos = cos_ref[...]
        sin = sin_ref[...]
        lane = lax.broadcasted_iota(jnp.int32, (m, LANES), 1)
        first_half = (lane % HEAD_DIM) < (HEAD_DIM // 2)

    def rotary(t, j):
        tb = t[:, j * LANES:(j + 1) * LANES]
        return _rope_block(tb, cos, sin, first_half) if use_rope else tb

    def chunk_tiled(ref, j):
        tab = ref[:, j * LANES:(j + 1) * LANES]
        return jnp.concatenate([tab] * (m // CHUNK), axis=0)

    kz_ref, v_ref = (outs[0], outs[1]) if kv_only else (outs[4], outs[2])
    k = _dot(h, w_ref[:, K0:V0])
    v = _dot(h, w_ref[:, V0:G0])
    for j in range(HEAD_PAIRS):
        kb = rotary(k, j) * (HEAD_DIM ** -0.5)
        if not kv_only:
            put_block(outs[1], j, kb)
        put_block(kz_ref, j, kb * chunk_tiled(zf_ref, j))
        put_block(kz_ref, HEAD_PAIRS + j, kb * chunk_tiled(zb_ref, j))
        put_block(v_ref, j, v[:, j * LANES:(j + 1) * LANES])
    if kv_only:
        return

    q_ref, _, _, g_ref, _, z_ref, pc_ref = outs
    q = _dot(h, w_ref[:, Q0:K0])
    g = _dot(h, w_ref[:, G0:F0])
    g = g * jax.nn.sigmoid(g)
    for j in range(HEAD_PAIRS):
        put_block(q_ref, j, rotary(q, j))
        put_block(g_ref, j, g[:, j * LANES:(j + 1) * LANES])
    fpc = _dot(h, w_ref[:, F0:IN_WIDTH])
    f = fpc[:, 0:FOURIER_WIDTH].astype(BF16)
    put(z_ref, 0, _dot(f, dft_ref[...]))
    put(pc_ref, 0, fpc[:, CB0 - F0:CB0 - F0 + CONV_WIDTH])
    put(pc_ref, CONV_WIDTH, fpc[:, CB0 - F0 + CONV_WIDTH:CB0 - F0 + 2 * CONV_WIDTH]
        * fpc[:, CB0 - F0 + 2 * CONV_WIDTH:])


def _in_proj(x, mod, w_in, tables, rope, dft_c, *, nb, tt, kv_only=False):
    b, t, d = x.shape
    use_rope = rope is not None
    shared_mod = mod.shape[0] == 1
    assert nb == 1 or (shared_mod and not use_rope)
    assert tt % CHUNK == 0
    mod_map = (lambda i, j: (0, 0, 0)) if shared_mod else (lambda i, j: (i, 0, 0))
    in_specs = [
        pl.BlockSpec((nb, tt, d), lambda i, j: (i, j, 0)),
        pl.BlockSpec((1, 1, mod.shape[2]), mod_map),
        _resident(w_in.shape),
        _resident(tables["zeta_f"].shape),
        _resident(tables["zeta_b"].shape),
    ]
    args = [x, mod, w_in, tables["zeta_f"], tables["zeta_b"]]
    if use_rope:
        in_specs += [pl.BlockSpec((tt, LANES), lambda i, j: (j, 0))] * 2
        args += list(rope)
    if kv_only:
        widths = (2 * RET_WIDTH, RET_WIDTH)
        blocked = (True, True)
    else:
        in_specs.append(_resident(dft_c.shape))
        args.append(dft_c)
        widths = (RET_WIDTH,) * 4 + (2 * RET_WIDTH, 2 * FOURIER_WIDTH, 2 * CONV_WIDTH)
        blocked = (True,) * 5 + (False, False)
    out_shape = tuple(jax.ShapeDtypeStruct((b, w // LANES, t, LANES) if blk else (b, t, w), BF16)
                      for w, blk in zip(widths, blocked))
    out_specs = tuple(pl.BlockSpec((nb, w // LANES, tt, LANES), lambda i, j: (i, 0, j, 0)) if blk
                      else pl.BlockSpec((nb, tt, w), lambda i, j: (i, j, 0))
                      for w, blk in zip(widths, blocked))
    return pl.pallas_call(
        functools.partial(_in_proj_kernel, d_model=d, use_rope=use_rope, kv_only=kv_only),
        out_shape=out_shape,
        grid=(b // nb, t // tt),
        in_specs=in_specs,
        out_specs=out_specs,
        compiler_params=_params(2),
        name="in_proj_kv" if kv_only else "in_proj",
    )(*args)


def _block_diag_mask():
    row = lax.broadcasted_iota(jnp.int32, (LANES, LANES), 0)
    col = lax.broadcasted_iota(jnp.int32, (LANES, LANES), 1)
    return (row < HEAD_DIM) == (col < HEAD_DIM)


def _state_sweep(kz_ref, v_ref, cdf_ref, cdb_ref, sf_ref, sb_ref, s_all, n_chunks):
    diag = _block_diag_mask()

    def update(state_ref, p, kz, v, decay):
        u = lax.dot_general(kz, v, (((0,), (0,)), ((), ())), preferred_element_type=F32)
        state_ref[p] = decay * state_ref[p] + jnp.where(diag, u, 0.0)

    def body(i, carry):
        rf = pl.multiple_of(i * CHUNK, CHUNK)
        nb_ = n_chunks - 1 - i
        rb = pl.multiple_of(nb_ * CHUNK, CHUNK)
        for p in range(HEAD_PAIRS):
            lanes = slice(p * LANES, (p + 1) * LANES)
            if s_all is not None:
                s_all[i, p, 0] = sf_ref[p].astype(BF16)
                s_all[nb_, p, 1] = sb_ref[p].astype(BF16)
            update(sf_ref, p, kz_ref[0, p, pl.ds(rf, CHUNK), :], v_ref[0, p, pl.ds(rf, CHUNK), :],
                   cdf_ref[:, lanes])
            update(sb_ref, p, kz_ref[0, HEAD_PAIRS + p, pl.ds(rb, CHUNK), :], v_ref[0, p, pl.ds(rb, CHUNK), :],
                   cdb_ref[:, lanes])
        return carry

    lax.fori_loop(0, n_chunks, body, 0, unroll=8 if n_chunks % 8 == 0 else (2 if n_chunks % 2 == 0 else 1))


def _retention_states_kernel(kz_ref, v_ref, cdf_ref, cdb_ref, sf_out, sb_out, *, t):
    sf_out[...] = jnp.zeros(sf_out.shape, F32)
    sb_out[...] = jnp.zeros(sb_out.shape, F32)
    _state_sweep(kz_ref, v_ref, cdf_ref, cdb_ref, sf_out.at[0], sb_out.at[0], None, t // CHUNK)


def _blocked_spec(a):
    return pl.BlockSpec((1,) + a.shape[1:], lambda i: (i, 0, 0, 0))


def _retention_states(kz, v, tables):
    b, _, t, _ = v.shape
    state = jax.ShapeDtypeStruct((b, HEAD_PAIRS, LANES, LANES), F32)
    ins = [tables["cd_f"], tables["cd_b"]]
    return pl.pallas_call(
        functools.partial(_retention_states_kernel, t=t),
        out_shape=(state, state),
        grid=(b,),
        in_specs=[_blocked_spec(kz), _blocked_spec(v)] + [_resident(a.shape) for a in ins],
        out_specs=(pl.BlockSpec((1, HEAD_PAIRS, LANES, LANES), lambda i: (i, 0, 0, 0)),) * 2,
        compiler_params=_params(1),
        name="retention_states",
    )(kz, v, *ins)


def _mixers_kernel(*refs, t, has_init, out_states):
    it = iter(refs)
    q_ref, k_ref, v_ref, g_ref, kz_ref, pc_ref = (next(it) for _ in range(6))
    dmask_ref, xif_ref, xib_ref, cdf_ref, cdb_ref, cw_ref = (next(it) for _ in range(6))
    if has_init:
        if_ref, ib_ref = next(it), next(it)
    ret_ref, conv_ref = next(it), next(it)
    if out_states:
        sf_out, sb_out = next(it), next(it)
    sf_ref, sb_ref, s_all = next(it), next(it), next(it)

    n_chunks = t // CHUNK
    if has_init:
        sf_ref[...] = if_ref[0]
        sb_ref[...] = ib_ref[0]
    else:
        sf_ref[...] = jnp.zeros(sf_ref.shape, F32)
        sb_ref[...] = jnp.zeros(sb_ref.shape, F32)

    _state_sweep(kz_ref, v_ref, cdf_ref, cdb_ref, sf_ref, sb_ref, s_all, n_chunks)
    if out_states:
        sf_out[0] = sf_ref[...]
        sb_out[0] = sb_ref[...]

    lane = lax.broadcasted_iota(jnp.int32, (1, LANES), 1)
    head0 = lane < HEAD_DIM
    m0 = jnp.where(head0, 1.0, 0.0).astype(BF16)
    m1 = jnp.where(head0, 0.0, 1.0).astype(BF16)
    w = cw_ref[...]
    a0 = jnp.where(head0, 1.0 / HEAD_DIM, 0.0)
    a1 = jnp.where(head0, 0.0, 1.0 / HEAD_DIM)

    def out_chunk(n, carry):
        r0 = pl.multiple_of(n * CHUNK, CHUNK)
        rows = pl.ds(r0, CHUNK)
        for p in range(HEAD_PAIRS):
            lanes = slice(p * LANES, (p + 1) * LANES)
            q = q_ref[0, p, rows, :]
            k = k_ref[0, p, rows, :]
            v = v_ref[0, p, rows, :]
            kk = jnp.concatenate([k * m0, k * m1], axis=0)
            vv = jnp.concatenate([v * m0, v * m1], axis=0)
            s = lax.dot_general(q, kk, (((1,), (1,)), ((), ())), preferred_element_type=F32)
            dmask = jnp.concatenate([dmask_ref[p, 0], dmask_ref[p, 1]], axis=1)
            pr = (s * dmask).astype(BF16)
            o = _dot(pr, vv)
            c = _dot(q, jnp.concatenate([s_all[n, p, 0], s_all[n, p, 1]], axis=1))
            o = o + c[:, 0:LANES] * xif_ref[:, lanes] + c[:, LANES:2 * LANES] * xib_ref[:, lanes]
            o2 = o * o + EPS
            r0n = lax.rsqrt(jnp.sum(o2 * a0, axis=-1, keepdims=True))
            r1n = lax.rsqrt(jnp.sum(o2 * a1, axis=-1, keepdims=True))
            o = o * jnp.where(head0, r0n, r1n)
            ret_ref[0, p, rows, :] = (o * g_ref[0, p, rows, :].astype(F32)).astype(ret_ref.dtype)

        def gate_prod(start, size):
            return pc_ref[0, pl.ds(start, size), CONV_WIDTH:2 * CONV_WIDTH].astype(F32)

        lo = pl.multiple_of(jnp.maximum(r0 - HALO, 0), HALO)
        hi = pl.multiple_of(jnp.minimum(r0 + CHUNK, t - HALO), HALO)
        before = jnp.where(n == 0, 0.0, gate_prod(lo, HALO))
        after = jnp.where(n == n_chunks - 1, 0.0, gate_prod(hi, HALO))
        mid = gate_prod(r0, CHUNK)
        ext = jnp.concatenate([before, mid, after], axis=0)
        acc = (ext[HALO - 1:HALO - 1 + CHUNK] * w[0:1, :] + mid * w[1:2, :]
               + ext[HALO + 1:HALO + 1 + CHUNK] * w[2:3, :])
        cb = pc_ref[0, rows, 0:CONV_WIDTH].astype(F32)
        conv_ref[0, rows, :] = (cb * acc).astype(conv_ref.dtype)
        return carry

    lax.fori_loop(0, n_chunks, out_chunk, 0, unroll=8 if n_chunks % 8 == 0 else 2)


def _mixers(q, k, v, g, kz, pc, tables, conv_w, init, *, out_states):
    b, _, t, _ = q.shape
    has_init = init is not None
    n_chunks = t // CHUNK
    tabs = [tables[k] for k in ("dmask", "xi_f", "xi_b", "cd_f", "cd_b")] + [conv_w]
    args = [q, k, v, g, kz, pc]
    in_specs = ([_blocked_spec(a) for a in args[:-1]] + [pl.BlockSpec((1, t, pc.shape[2]), lambda i: (i, 0, 0))]
                + [_resident(a.shape) for a in tabs])
    args = args + tabs
    state_spec = pl.BlockSpec((1, HEAD_PAIRS, LANES, LANES), lambda i: (i, 0, 0, 0))
    if has_init:
        in_specs += [state_spec, state_spec]
        args += list(init)
    out_shape = [jax.ShapeDtypeStruct((b, HEAD_PAIRS, t, LANES), BF16),
                 jax.ShapeDtypeStruct((b, t, CONV_WIDTH), BF16)]
    out_specs = [pl.BlockSpec((1, HEAD_PAIRS, t, LANES), lambda i: (i, 0, 0, 0)),
                 pl.BlockSpec((1, t, CONV_WIDTH), lambda i: (i, 0, 0))]
    if out_states:
        out_shape += [jax.ShapeDtypeStruct((b, HEAD_PAIRS, LANES, LANES), F32)] * 2
        out_specs += [state_spec, state_spec]
    return pl.pallas_call(
        functools.partial(_mixers_kernel, t=t, has_init=has_init, out_states=out_states),
        out_shape=tuple(out_shape),
        grid=(b,),
        in_specs=in_specs,
        out_specs=tuple(out_specs),
        scratch_shapes=[
            pltpu.VMEM((HEAD_PAIRS, LANES, LANES), F32),
            pltpu.VMEM((HEAD_PAIRS, LANES, LANES), F32),
            pltpu.VMEM((n_chunks, HEAD_PAIRS, 2, LANES, LANES), BF16),
        ],
        compiler_params=_params(1),
        name="mixers",
    )(*args)


def _out_proj_kernel(x_ref, mod_ref, ret_ref, conv_ref, z_ref, cm_ref, sm_ref, w_ref, o_ref, h_ref,
                     *, d_model, dft_scale):
    nb, tt, d = x_ref.shape
    gate = mod_ref[0, :, 2 * d_model:3 * d_model]
    shift = mod_ref[0, :, 3 * d_model:4 * d_model]
    scale = mod_ref[0, :, 4 * d_model:5 * d_model]
    rb = min(tt, OUT_PROJ_ROWS)
    for n in range(nb):
        zr = z_ref[n, :, 0:FOURIER_WIDTH]
        zi = z_ref[n, :, FOURIER_WIDTH:2 * FOURIER_WIDTH]
        for r in range(tt // rb):
            rows = slice(r * rb, (r + 1) * rb)
            four = ((_dot(cm_ref[rows, :], zr) - _dot(sm_ref[rows, :], zi)) * dft_scale).astype(BF16)
            ret = jnp.concatenate([ret_ref[n, p, rows, :] for p in range(HEAD_PAIRS)], axis=1)
            acc = _dot(ret, w_ref[0:RET_WIDTH, :])
            acc = acc + _dot(four, w_ref[RET_WIDTH:RET_WIDTH + FOURIER_WIDTH, :])
            acc = acc + _dot(conv_ref[n, rows, :], w_ref[RET_WIDTH + FOURIER_WIDTH:, :])
            y = x_ref[n, rows, :] + gate * acc
            o_ref[n, rows, :] = y
            h_ref[n, rows, :] = _mod_norm(y, shift, scale).astype(h_ref.dtype)


def _out_proj(x, mod, ret, conv, z, cos_m, sin_m, w_out, *, nb, tt):
    b, t, d = x.shape
    shared_mod = mod.shape[0] == 1
    assert nb == 1 or (shared_mod and tt == t)
    mod_map = (lambda j, i: (0, 0, 0)) if shared_mod else (lambda j, i: (i, 0, 0))
    blk = lambda w: pl.BlockSpec((nb, tt, w), lambda j, i: (i, j, 0))
    dft_blk = pl.BlockSpec((tt, t), lambda j, i: (j, 0), pipeline_mode=pl.Buffered(1))
    dft_scale = float(1.0 / np.sqrt(float(t) * FOURIER_GROUP_DIM))
    return pl.pallas_call(
        functools.partial(_out_proj_kernel, d_model=d, dft_scale=dft_scale),
        out_shape=(jax.ShapeDtypeStruct(x.shape, F32), jax.ShapeDtypeStruct(x.shape, BF16)),
        grid=(t // tt, b // nb),
        in_specs=[blk(d), pl.BlockSpec((1, 1, mod.shape[2]), mod_map),
                  pl.BlockSpec((nb, HEAD_PAIRS, tt, LANES), lambda j, i: (i, 0, j, 0)), blk(CONV_WIDTH),
                  pl.BlockSpec((nb, t, z.shape[2]), lambda j, i: (i, 0, 0)),
                  dft_blk, dft_blk, _resident(w_out.shape)],
        out_specs=(blk(d), blk(d)),
        compiler_params=_params(2),
        name="out_proj",
    )(x, mod, ret, conv, z, cos_m, sin_m, w_out)


def _ffn_kernel(*refs, d_model, hidden, fh, halo, final_norm):
    it = iter(refs)
    x_ref, h_ref = next(it), next(it)
    hp_ref = hn_ref = None
    if halo:
        hp_ref, hn_ref = next(it), next(it)
    mod_ref, wu_ref, cw_ref, wd_ref = (next(it) for _ in range(4))
    fg_ref = next(it) if final_norm else None
    o_ref, he_ref, act_ref = next(it), next(it), next(it)

    _, tt, d = x_ref.shape
    i = pl.program_id(1)
    he_ref[0:tt, :] = jnp.swapaxes(h_ref[0].reshape(HALO, tt // HALO, d), 0, 1).reshape(tt, d)
    if halo:
        rid = lax.broadcasted_iota(jnp.int32, (HALO, d), 0)
        before = jnp.where(i == 0, 0.0, hp_ref[0].astype(F32)[HALO - 1:HALO, :])
        after = jnp.where(i == pl.num_programs(1) - 1, 0.0, hn_ref[0].astype(F32)[0:1, :])
        edge = jnp.where(rid == 0, before, jnp.where(rid == 1, after, 0.0))
    else:
        edge = jnp.zeros((HALO, d), F32)
    he_ref[tt:tt + HALO, :] = edge.astype(BF16)

    def conv_rows(u, w):
        main = u[0:tt]
        first_prev = jnp.concatenate([u[tt:tt + 1], main[tt - HALO:tt - 1]], axis=0)
        last_next = jnp.concatenate([main[1:HALO], u[tt + 1:tt + 2]], axis=0)
        prev = jnp.concatenate([first_prev, main[0:tt - HALO]], axis=0)
        nxt = jnp.concatenate([main[HALO:tt], last_next], axis=0)
        return prev * w[0:1, :] + main * w[1:2, :] + nxt * w[2:3, :]

    for j in range(hidden // fh):
        vc = slice(j * fh, (j + 1) * fh)
        gc = slice(hidden + j * fh, hidden + (j + 1) * fh)
        he = he_ref[...]
        val = conv_rows(_dot(he, wu_ref[:, vc]), cw_ref[:, vc])
        gate = conv_rows(_dot(he, wu_ref[:, gc]), cw_ref[:, gc])
        act_ref[:, vc] = (val * (gate * jax.nn.sigmoid(gate))).astype(BF16)

    gate2 = mod_ref[0, :, 5 * d_model:6 * d_model]
    down = _dot(act_ref[...], wd_ref[...])
    y = x_ref[0] + gate2 * jnp.swapaxes(down.reshape(tt // HALO, HALO, d), 0, 1).reshape(tt, d)
    if final_norm:
        ms = jnp.mean(y * y, axis=-1, keepdims=True)
        y = (y * lax.rsqrt(ms + EPS)) * fg_ref[...]
    o_ref[0] = y


def _ffn(x, h, mod, w_up, conv_w, w_down, final_g, *, tt, fh):
    b, t, d = x.shape
    hidden = w_down.shape[0]
    assert hidden % fh == 0 and t % tt == 0 and tt % HALO == 0
    halo = tt < t
    shared_mod = mod.shape[0] == 1
    mod_map = (lambda bi, i: (0, 0, 0)) if shared_mod else (lambda bi, i: (bi, 0, 0))
    final_norm = final_g is not None
    tile = pl.BlockSpec((1, tt, d), lambda bi, i: (bi, i, 0))
    in_specs = [tile, tile]
    args = [x, h]
    if halo:
        per = tt // HALO
        last = t // HALO - 1
        in_specs += [
            pl.BlockSpec((1, HALO, d), lambda bi, i: (bi, jnp.maximum(i * per - 1, 0), 0)),
            pl.BlockSpec((1, HALO, d), lambda bi, i: (bi, jnp.minimum((i + 1) * per, last), 0)),
        ]
        args += [h, h]
    in_specs += [pl.BlockSpec((1, 1, mod.shape[2]), mod_map),
                 _resident(w_up.shape), _resident(conv_w.shape), _resident(w_down.shape)]
    args += [mod, w_up, conv_w, w_down]
    if final_norm:
        in_specs.append(_resident((1, d)))
        args.append(final_g.reshape(1, d))
    return pl.pallas_call(
        functools.partial(_ffn_kernel, d_model=d, hidden=hidden, fh=fh, halo=halo, final_norm=final_norm),
        out_shape=jax.ShapeDtypeStruct(x.shape, F32),
        grid=(b, t // tt),
        in_specs=in_specs,
        out_specs=tile,
        scratch_shapes=[pltpu.VMEM((tt + HALO, d), BF16), pltpu.VMEM((tt, hidden), BF16)],
        compiler_params=_params(2),
        name="ffn",
    )(*args)


def _rope_tables(t):
    rows = t // GRID_W
    row = jnp.repeat(jnp.arange(rows, dtype=F32), GRID_W)
    col = jnp.tile(jnp.arange(GRID_W, dtype=F32), rows)
    n_freq = HEAD_DIM // 4
    freq = ROPE_BASE ** (-jnp.arange(n_freq, dtype=F32) / n_freq)
    ang = jnp.concatenate([row[:, None] * freq, col[:, None] * freq], axis=-1)
    reps = LANES // ang.shape[1]
    cos = jnp.tile(jnp.cos(ang), (1, reps))
    sign = jnp.where((jnp.arange(LANES) % HEAD_DIM) < HEAD_DIM // 2, -1.0, 1.0).astype(F32)
    sin = jnp.tile(jnp.sin(ang), (1, reps)) * sign
    return cos, sin


def _decay_tables(decay_logit):
    ld = jax.nn.log_sigmoid(decay_logit.astype(F32))
    ld_f, ld_b = ld[0], ld[1]
    i = jnp.arange(CHUNK, dtype=F32)
    lane_f = jnp.repeat(ld_f, HEAD_DIM)[None, :]
    lane_b = jnp.repeat(ld_b, HEAD_DIM)[None, :]
    diff = i[:, None] - i[None, :]
    dmask = jnp.where(diff > 0, jnp.exp(ld_f[:, None, None] * jnp.maximum(diff, 0.0)),
                      jnp.where(diff < 0, jnp.exp(ld_b[:, None, None] * jnp.maximum(-diff, 0.0)), 2.0))
    return {
        "dmask": dmask.reshape(HEAD_PAIRS, 2, CHUNK, CHUNK),
        "xi_f": jnp.exp(lane_f * (i[:, None] + 1.0)),
        "xi_b": jnp.exp(lane_b * (CHUNK - i[:, None])),
        "zeta_f": jnp.exp(lane_f * (CHUNK - 1.0 - i[:, None])),
        "zeta_b": jnp.exp(lane_b * i[:, None]),
        "cd_f": jnp.exp(lane_f * CHUNK),
        "cd_b": jnp.exp(lane_b * CHUNK),
    }


def _position_dft(t):
    blk = GRID_W if t % GRID_W == 0 else 1
    r = jnp.arange(t, dtype=jnp.int32)[:, None]

    def table(cols):
        ang = ((r * cols[None, :]) % t).astype(F32) * (2.0 * np.pi / t)
        return jnp.cos(ang)[:, :, None], jnp.sin(ang)[:, :, None]

    ca, sa = table(jnp.arange(t // blk, dtype=jnp.int32) * blk)
    cb, sb = table(jnp.arange(blk, dtype=jnp.int32))
    cb, sb = cb.reshape(t, 1, blk), sb.reshape(t, 1, blk)
    cos = (ca * cb - sa * sb).reshape(t, t)
    sin = (sa * cb + ca * sb).reshape(t, t)
    return cos.astype(BF16), sin.astype(BF16)


def _channel_dft():
    n = FOURIER_GROUP_DIM
    idx = jnp.arange(n, dtype=jnp.int32)
    ang = ((idx[:, None] * idx[None, :]) % n).astype(F32) * (2.0 * np.pi / n)
    eye = jnp.eye(FOURIER_GROUPS, dtype=F32)
    return jnp.concatenate([jnp.kron(eye, jnp.cos(ang)), jnp.kron(eye, jnp.sin(ang))], axis=1).astype(BF16)


def kernel(x, c, ctx, c_ctx, w_mod, b_mod, w_in, ret_decay_logit, mix_conv_w, w_out,
           ffn_w_up, ffn_conv_w, ffn_w_down, final_norm_g):
    b, t, d = x.shape
    _, lc, _ = ctx.shape
    depth = w_mod.shape[0]

    rows = ((b + 1 + 7) // 8) * 8
    c_all = jnp.zeros((rows, d), F32).at[:b].set(c).at[b].set(c_ctx)
    mod_all = _modulation(c_all, w_mod, b_mod)

    rope = _rope_tables(t)
    dft_c = _channel_dft()
    dft_x = _position_dft(t)
    dft_ctx = _position_dft(lc)

    tt = min(t, 512)
    tt_big = min(t, 1024)
    nb_ctx = max(1, min(b, 1024 // lc))
    fh = 256

    for l in range(depth):
        last = l == depth - 1
        mod_x = mod_all[l, :b].reshape(b, 1, 6 * d)
        mod_c = mod_all[l, b:b + 1].reshape(1, 1, 6 * d)
        w_in_l = w_in[l].astype(BF16)
        w_out_l = w_out[l].astype(BF16)
        w_up_l = ffn_w_up[l].astype(BF16)
        w_down_l = ffn_w_down[l].astype(BF16)
        tables = _decay_tables(ret_decay_logit[l])

        if last:
            kz_c, v_c = _in_proj(ctx, mod_c, w_in_l, tables, None, dft_c, nb=nb_ctx, tt=lc, kv_only=True)
            st_f, st_b = _retention_states(kz_c, v_c, tables)
        else:
            *qkvgz_c, z_c, pc_c = _in_proj(ctx, mod_c, w_in_l, tables, None, dft_c, nb=nb_ctx, tt=lc)
            ret_c, conv_c, st_f, st_b = _mixers(*qkvgz_c, pc_c, tables, mix_conv_w[l], None, out_states=True)

        *qkvgz, z, pc = _in_proj(x, mod_x, w_in_l, tables, rope, dft_c, nb=1, tt=tt_big)
        ret, conv = _mixers(*qkvgz, pc, tables, mix_conv_w[l], (st_f, st_b), out_states=False)
        x, h = _out_proj(x, mod_x, ret, conv, z, *dft_x, w_out_l, nb=1, tt=tt_big)
        x = _ffn(x, h, mod_x, w_up_l, ffn_conv_w[l], w_down_l, final_norm_g if last else None, tt=tt_big, fh=fh)

        if not last:
            ctx, h_c = _out_proj(ctx, mod_c, ret_c, conv_c, z_c, *dft_ctx, w_out_l, nb=nb_ctx, tt=lc)
            ctx = _ffn(ctx, h_c, mod_c, w_up_l, ffn_conv_w[l], w_down_l, None, tt=lc, fh=fh)
    return x
```

```python
import functools

import jax
import jax.numpy as jnp
import numpy as np
from jax import lax
from jax.experimental import pallas as pl
from jax.experimental.pallas import tpu as pltpu

F32 = jnp.float32
BF16 = jnp.bfloat16

GRID_W = 64
RET_HEADS = 8
HEAD_DIM = 64
RET_WIDTH = RET_HEADS * HEAD_DIM
FOURIER_GROUPS = 4
FOURIER_GROUP_DIM = 64
FOURIER_WIDTH = FOURIER_GROUPS * FOURIER_GROUP_DIM
CONV_WIDTH = 256
ROPE_BASE = 10000.0
EPS = 1e-6
CHUNK = 128

LANES = 128
BF16_SUBLANES = 16
V7X_VMEM_LIMIT_BYTES = 60000 * 1024

HEAD_PAIRS = RET_WIDTH // LANES
HALO = BF16_SUBLANES
OUT_PROJ_ROWS = 512

Q0 = 0
K0 = Q0 + RET_WIDTH
V0 = K0 + RET_WIDTH
G0 = V0 + RET_WIDTH
F0 = G0 + RET_WIDTH
CB0 = F0 + FOURIER_WIDTH
IN_WIDTH = CB0 + 3 * CONV_WIDTH


def _params(n_grid_dims, vmem=V7X_VMEM_LIMIT_BYTES):
    return pltpu.CompilerParams(
        dimension_semantics=("arbitrary",) * n_grid_dims,
        vmem_limit_bytes=vmem,
    )


def _dot(a, b):
    return jnp.dot(a, b, preferred_element_type=F32)


def _mod_norm(x, shift, scale):
    ms = jnp.mean(x * x, axis=-1, keepdims=True)
    return (x * lax.rsqrt(ms + EPS)) * (1.0 + scale) + shift


def _resident(shape):
    return pl.BlockSpec(shape, lambda *_: (0,) * len(shape), pipeline_mode=pl.Buffered(1))


def _mod_kernel(c_ref, w_ref, b_ref, o_ref):
    c = c_ref[...]
    s = c * jax.nn.sigmoid(c)
    o_ref[0] = jnp.dot(s, w_ref[0], preferred_element_type=F32,
                       precision=lax.Precision.HIGHEST) + b_ref[0]


def _modulation(c_all, w_mod, b_mod):
    depth, d, n = w_mod.shape
    rows = c_all.shape[0]
    tn = 1024
    return pl.pallas_call(
        _mod_kernel,
        out_shape=jax.ShapeDtypeStruct((depth, rows, n), F32),
        grid=(depth, n // tn),
        in_specs=[
            pl.BlockSpec((rows, d), lambda l, j: (0, 0)),
            pl.BlockSpec((1, d, tn), lambda l, j: (l, 0, j)),
            pl.BlockSpec((1, 1, tn), lambda l, j: (l, 0, j)),
        ],
        out_specs=pl.BlockSpec((1, rows, tn), lambda l, j: (l, 0, j)),
        compiler_params=_params(2),
        name="modulation",
    )(c_all, w_mod, b_mod.reshape(depth, 1, n))


def _rope_block(t, cos, sin, first_half):
    lo = pltpu.roll(t, LANES - HEAD_DIM // 2, 1)
    hi = pltpu.roll(t, HEAD_DIM // 2, 1)
    return t * cos + jnp.where(first_half, lo, hi) * sin


def _in_proj_kernel(*refs, d_model, use_rope, kv_only):
    it = iter(refs)
    x_ref, mod_ref, w_ref, zf_ref, zb_ref = (next(it) for _ in range(5))
    cos_ref = sin_ref = dft_ref = None
    if use_rope:
        cos_ref, sin_ref = next(it), next(it)
    if not kv_only:
        dft_ref = next(it)
    outs = list(it)

    nb, tt, d = x_ref.shape
    m = nb * tt
    x = x_ref[...].reshape(m, d)
    shift = mod_ref[0, :, 0:d_model]
    scale = mod_ref[0, :, d_model:2 * d_model]
    h = _mod_norm(x, shift, scale).astype(BF16)

    def put(ref, col, val):
        ref[:, :, col:col + val.shape[1]] = val.astype(ref.dtype).reshape(nb, tt, val.shape[1])

    def put_block(ref, j, val):
        ref[:, j, :, :] = val.astype(ref.dtype).reshape(nb, tt, LANES)

    if use_rope:
        cos = cos_ref[...]
        sin = sin_ref[...]
        lane = lax.broadcasted_iota(jnp.int32, (m, LANES), 1)
        first_half = (lane % HEAD_DIM) < (HEAD_DIM // 2)

    def rotary(t, j):
        tb = t[:, j * LANES:(j + 1) * LANES]
        return _rope_block(tb, cos, sin, first_half) if use_rope else tb

    def chunk_tiled(ref, j):
        return jnp.concatenate([ref[j]] * (m // CHUNK), axis=0)

    kz_ref, v_ref = (outs[0], outs[1]) if kv_only else (outs[4], outs[2])
    k = _dot(h, w_ref[:, K0:V0])
    v = _dot(h, w_ref[:, V0:G0])
    for j in range(HEAD_PAIRS):
        kb = rotary(k, j) * (HEAD_DIM ** -0.5)
        if not kv_only:
            put_block(outs[1], j, kb)
        put_block(kz_ref, j, kb * chunk_tiled(zf_ref, j))
        put_block(kz_ref, HEAD_PAIRS + j, kb * chunk_tiled(zb_ref, j))
        put_block(v_ref, j, v[:, j * LANES:(j + 1) * LANES])
    if kv_only:
        return

    q_ref, _, _, g_ref, _, z_ref, pc_ref = outs
    q = _dot(h, w_ref[:, Q0:K0])
    g = _dot(h, w_ref[:, G0:F0])
    g = g * jax.nn.sigmoid(g)
    for j in range(HEAD_PAIRS):
        put_block(q_ref, j, rotary(q, j))
        put_block(g_ref, j, g[:, j * LANES:(j + 1) * LANES])
    fpc = _dot(h, w_ref[:, F0:IN_WIDTH])
    f = fpc[:, 0:FOURIER_WIDTH].astype(BF16)
    put(z_ref, 0, _dot(f, dft_ref[...]))
    cb = fpc[:, CB0 - F0:CB0 - F0 + CONV_WIDTH]
    gp = fpc[:, CB0 - F0 + CONV_WIDTH:CB0 - F0 + 2 * CONV_WIDTH] * fpc[:, CB0 - F0 + 2 * CONV_WIDTH:]
    for j in range(CONV_WIDTH // LANES):
        put_block(pc_ref, j, cb[:, j * LANES:(j + 1) * LANES])
        put_block(pc_ref, CONV_WIDTH // LANES + j, gp[:, j * LANES:(j + 1) * LANES])


def _in_proj(x, mod, w_in, tables, rope, dft_c, *, nb, tt, kv_only=False):
    b, t, d = x.shape
    use_rope = rope is not None
    shared_mod = mod.shape[0] == 1
    assert nb == 1 or (shared_mod and not use_rope)
    assert tt % CHUNK == 0
    mod_map = (lambda i, j: (0, 0, 0)) if shared_mod else (lambda i, j: (i, 0, 0))
    in_specs = [
        pl.BlockSpec((nb, tt, d), lambda i, j: (i, j, 0)),
        pl.BlockSpec((1, 1, mod.shape[2]), mod_map),
        _resident(w_in.shape),
        _resident(tables["zeta_f"].shape),
        _resident(tables["zeta_b"].shape),
    ]
    args = [x, mod, w_in, tables["zeta_f"], tables["zeta_b"]]
    if use_rope:
        in_specs += [pl.BlockSpec((tt, LANES), lambda i, j: (j, 0))] * 2
        args += list(rope)
    if kv_only:
        widths = (2 * RET_WIDTH, RET_WIDTH)
        blocked = (True, True)
    else:
        in_specs.append(_resident(dft_c.shape))
        args.append(dft_c)
        widths = (RET_WIDTH,) * 4 + (2 * RET_WIDTH, 2 * FOURIER_WIDTH, 2 * CONV_WIDTH)
        blocked = (True,) * 5 + (False, True)
    out_shape = tuple(jax.ShapeDtypeStruct((b, w // LANES, t, LANES) if blk else (b, t, w), BF16)
                      for w, blk in zip(widths, blocked))
    out_specs = tuple(pl.BlockSpec((nb, w // LANES, tt, LANES), lambda i, j: (i, 0, j, 0)) if blk
                      else pl.BlockSpec((nb, tt, w), lambda i, j: (i, j, 0))
                      for w, blk in zip(widths, blocked))
    return pl.pallas_call(
        functools.partial(_in_proj_kernel, d_model=d, use_rope=use_rope, kv_only=kv_only),
        out_shape=out_shape,
        grid=(b // nb, t // tt),
        in_specs=in_specs,
        out_specs=out_specs,
        compiler_params=_params(2),
        name="in_proj_kv" if kv_only else "in_proj",
    )(*args)


def _block_diag_mask():
    row = lax.broadcasted_iota(jnp.int32, (LANES, LANES), 0)
    col = lax.broadcasted_iota(jnp.int32, (LANES, LANES), 1)
    return (row < HEAD_DIM) == (col < HEAD_DIM)


def _state_sweep(kz_ref, v_ref, cdf_ref, cdb_ref, sf_ref, sb_ref, s_all, n_chunks):
    diag = _block_diag_mask()

    def update(state_ref, p, kz, v, decay):
        u = lax.dot_general(kz, v, (((0,), (0,)), ((), ())), preferred_element_type=F32)
        state_ref[p] = decay * state_ref[p] + jnp.where(diag, u, 0.0)

    def body(i, carry):
        rf = pl.multiple_of(i * CHUNK, CHUNK)
        nb_ = n_chunks - 1 - i
        rb = pl.multiple_of(nb_ * CHUNK, CHUNK)
        for p in range(HEAD_PAIRS):
            lanes = slice(p * LANES, (p + 1) * LANES)
            if s_all is not None:
                s_all[i, p, 0] = sf_ref[p].astype(BF16)
                s_all[nb_, p, 1] = sb_ref[p].astype(BF16)
            update(sf_ref, p, kz_ref[0, p, pl.ds(rf, CHUNK), :], v_ref[0, p, pl.ds(rf, CHUNK), :],
                   cdf_ref[:, lanes])
            update(sb_ref, p, kz_ref[0, HEAD_PAIRS + p, pl.ds(rb, CHUNK), :], v_ref[0, p, pl.ds(rb, CHUNK), :],
                   cdb_ref[:, lanes])
        return carry

    lax.fori_loop(0, n_chunks, body, 0, unroll=8 if n_chunks % 8 == 0 else (2 if n_chunks % 2 == 0 else 1))


def _retention_states_kernel(kz_ref, v_ref, cdf_ref, cdb_ref, sf_out, sb_out, *, t):
    sf_out[...] = jnp.zeros(sf_out.shape, F32)
    sb_out[...] = jnp.zeros(sb_out.shape, F32)
    _state_sweep(kz_ref, v_ref, cdf_ref, cdb_ref, sf_out.at[0], sb_out.at[0], None, t // CHUNK)


def _blocked_spec(a):
    return pl.BlockSpec((1,) + a.shape[1:], lambda i: (i, 0, 0, 0))


def _retention_states(kz, v, tables):
    b, _, t, _ = v.shape
    state = jax.ShapeDtypeStruct((b, HEAD_PAIRS, LANES, LANES), F32)
    ins = [tables["cd_f"], tables["cd_b"]]
    return pl.pallas_call(
        functools.partial(_retention_states_kernel, t=t),
        out_shape=(state, state),
        grid=(b,),
        in_specs=[_blocked_spec(kz), _blocked_spec(v)] + [_resident(a.shape) for a in ins],
        out_specs=(pl.BlockSpec((1, HEAD_PAIRS, LANES, LANES), lambda i: (i, 0, 0, 0)),) * 2,
        compiler_params=_params(1),
        name="retention_states",
    )(kz, v, *ins)


def _mixers_kernel(*refs, t, has_init, out_states):
    it = iter(refs)
    q_ref, k_ref, v_ref, g_ref, kz_ref, pc_ref = (next(it) for _ in range(6))
    dmask_ref, xif_ref, xib_ref, cdf_ref, cdb_ref, cw_ref = (next(it) for _ in range(6))
    if has_init:
        if_ref, ib_ref = next(it), next(it)
    ret_ref, conv_ref = next(it), next(it)
    if out_states:
        sf_out, sb_out = next(it), next(it)
    sf_ref, sb_ref, s_all = next(it), next(it), next(it)

    n_chunks = t // CHUNK
    if has_init:
        sf_ref[...] = if_ref[0]
        sb_ref[...] = ib_ref[0]
    else:
        sf_ref[...] = jnp.zeros(sf_ref.shape, F32)
        sb_ref[...] = jnp.zeros(sb_ref.shape, F32)

    _state_sweep(kz_ref, v_ref, cdf_ref, cdb_ref, sf_ref, sb_ref, s_all, n_chunks)
    if out_states:
        sf_out[0] = sf_ref[...]
        sb_out[0] = sb_ref[...]

    lane = lax.broadcasted_iota(jnp.int32, (1, LANES), 1)
    head0 = lane < HEAD_DIM
    m0 = jnp.where(head0, 1.0, 0.0).astype(BF16)
    m1 = jnp.where(head0, 0.0, 1.0).astype(BF16)
    w = cw_ref[...]
    a0 = jnp.where(head0, 1.0 / HEAD_DIM, 0.0)
    a1 = jnp.where(head0, 0.0, 1.0 / HEAD_DIM)

    def out_chunk(n, carry):
        r0 = pl.multiple_of(n * CHUNK, CHUNK)
        rows = pl.ds(r0, CHUNK)
        for p in range(HEAD_PAIRS):
            lanes = slice(p * LANES, (p + 1) * LANES)
            q = q_ref[0, p, rows, :]
            k = k_ref[0, p, rows, :]
            v = v_ref[0, p, rows, :]
            kk = jnp.concatenate([k * m0, k * m1], axis=0)
            vv = jnp.concatenate([v * m0, v * m1], axis=0)
            s = lax.dot_general(q, kk, (((1,), (1,)), ((), ())), preferred_element_type=F32)
            dmask = jnp.concatenate([dmask_ref[p, 0], dmask_ref[p, 1]], axis=1)
            pr = (s * dmask).astype(BF16)
            o = _dot(pr, vv)
            c = _dot(q, jnp.concatenate([s_all[n, p, 0], s_all[n, p, 1]], axis=1))
            o = o + c[:, 0:LANES] * xif_ref[p] + c[:, LANES:2 * LANES] * xib_ref[p]
            o2 = o * o + EPS
            r0n = lax.rsqrt(jnp.sum(o2 * a0, axis=-1, keepdims=True))
            r1n = lax.rsqrt(jnp.sum(o2 * a1, axis=-1, keepdims=True))
            o = o * jnp.where(head0, r0n, r1n)
            ret_ref[0, p, rows, :] = (o * g_ref[0, p, rows, :].astype(F32)).astype(ret_ref.dtype)

        def conv_rows(first, start, size):
            return jnp.concatenate([pc_ref[0, first + j, pl.ds(start, size), :]
                                    for j in range(CONV_WIDTH // LANES)], axis=1).astype(F32)

        def gate_prod(start, size):
            return conv_rows(CONV_WIDTH // LANES, start, size)

        lo = pl.multiple_of(jnp.maximum(r0 - HALO, 0), HALO)
        hi = pl.multiple_of(jnp.minimum(r0 + CHUNK, t - HALO), HALO)
        before = jnp.where(n == 0, 0.0, gate_prod(lo, HALO))
        after = jnp.where(n == n_chunks - 1, 0.0, gate_prod(hi, HALO))
        mid = gate_prod(r0, CHUNK)
        ext = jnp.concatenate([before, mid, after], axis=0)
        acc = (ext[HALO - 1:HALO - 1 + CHUNK] * w[0:1, :] + mid * w[1:2, :]
               + ext[HALO + 1:HALO + 1 + CHUNK] * w[2:3, :])
        conv = (conv_rows(0, r0, CHUNK) * acc).astype(conv_ref.dtype)
        for j in range(CONV_WIDTH // LANES):
            conv_ref[0, j, rows, :] = conv[:, j * LANES:(j + 1) * LANES]
        return carry

    lax.fori_loop(0, n_chunks, out_chunk, 0, unroll=8 if n_chunks % 8 == 0 else 2)


def _mixers(q, k, v, g, kz, pc, tables, conv_w, init, *, out_states):
    b, _, t, _ = q.shape
    has_init = init is not None
    n_chunks = t // CHUNK
    tabs = [tables[k] for k in ("dmask", "xi_f", "xi_b", "cd_f", "cd_b")] + [conv_w]
    args = [q, k, v, g, kz, pc]
    in_specs = [_blocked_spec(a) for a in args] + [_resident(a.shape) for a in tabs]
    args = args + tabs
    state_spec = pl.BlockSpec((1, HEAD_PAIRS, LANES, LANES), lambda i: (i, 0, 0, 0))
    if has_init:
        in_specs += [state_spec, state_spec]
        args += list(init)
    out_shape = [jax.ShapeDtypeStruct((b, HEAD_PAIRS, t, LANES), BF16),
                 jax.ShapeDtypeStruct((b, CONV_WIDTH // LANES, t, LANES), BF16)]
    out_specs = [pl.BlockSpec((1, HEAD_PAIRS, t, LANES), lambda i: (i, 0, 0, 0)),
                 pl.BlockSpec((1, CONV_WIDTH // LANES, t, LANES), lambda i: (i, 0, 0, 0))]
    if out_states:
        out_shape += [jax.ShapeDtypeStruct((b, HEAD_PAIRS, LANES, LANES), F32)] * 2
        out_specs += [state_spec, state_spec]
    return pl.pallas_call(
        functools.partial(_mixers_kernel, t=t, has_init=has_init, out_states=out_states),
        out_shape=tuple(out_shape),
        grid=(b,),
        in_specs=in_specs,
        out_specs=tuple(out_specs),
        scratch_shapes=[
            pltpu.VMEM((HEAD_PAIRS, LANES, LANES), F32),
            pltpu.VMEM((HEAD_PAIRS, LANES, LANES), F32),
            pltpu.VMEM((n_chunks, HEAD_PAIRS, 2, LANES, LANES), BF16),
        ],
        compiler_params=_params(1),
        name="mixers",
    )(*args)


def _out_proj_kernel(x_ref, mod_ref, ret_ref, conv_ref, z_ref, cm_ref, sm_ref, w_ref, o_ref, h_ref,
                     *, d_model, dft_scale):
    nb, tt, d = x_ref.shape
    gate = mod_ref[0, :, 2 * d_model:3 * d_model]
    shift = mod_ref[0, :, 3 * d_model:4 * d_model]
    scale = mod_ref[0, :, 4 * d_model:5 * d_model]
    rb = min(tt, OUT_PROJ_ROWS)
    for n in range(nb):
        zr = z_ref[n, :, 0:FOURIER_WIDTH]
        zi = z_ref[n, :, FOURIER_WIDTH:2 * FOURIER_WIDTH]
        for r in range(tt // rb):
            rows = slice(r * rb, (r + 1) * rb)
            four = ((_dot(cm_ref[rows, :], zr) - _dot(sm_ref[rows, :], zi)) * dft_scale).astype(BF16)
            ret = jnp.concatenate([ret_ref[n, p, rows, :] for p in range(HEAD_PAIRS)], axis=1)
            acc = _dot(ret, w_ref[0:RET_WIDTH, :])
            acc = acc + _dot(four, w_ref[RET_WIDTH:RET_WIDTH + FOURIER_WIDTH, :])
            conv = jnp.concatenate([conv_ref[n, j, rows, :] for j in range(CONV_WIDTH // LANES)], axis=1)
            acc = acc + _dot(conv, w_ref[RET_WIDTH + FOURIER_WIDTH:, :])
            y = x_ref[n, rows, :] + gate * acc
            o_ref[n, rows, :] = y
            h_ref[n, rows, :] = _mod_norm(y, shift, scale).astype(h_ref.dtype)


def _out_proj(x, mod, ret, conv, z, cos_m, sin_m, w_out, *, nb, tt):
    b, t, d = x.shape
    shared_mod = mod.shape[0] == 1
    assert nb == 1 or (shared_mod and tt == t)
    mod_map = (lambda j, i: (0, 0, 0)) if shared_mod else (lambda j, i: (i, 0, 0))
    blk = lambda w: pl.BlockSpec((nb, tt, w), lambda j, i: (i, j, 0))
    dft_blk = pl.BlockSpec((tt, t), lambda j, i: (j, 0), pipeline_mode=pl.Buffered(1))
    dft_scale = float(1.0 / np.sqrt(float(t) * FOURIER_GROUP_DIM))
    return pl.pallas_call(
        functools.partial(_out_proj_kernel, d_model=d, dft_scale=dft_scale),
        out_shape=(jax.ShapeDtypeStruct(x.shape, F32), jax.ShapeDtypeStruct(x.shape, BF16)),
        grid=(t // tt, b // nb),
        in_specs=[blk(d), pl.BlockSpec((1, 1, mod.shape[2]), mod_map),
                  pl.BlockSpec((nb, HEAD_PAIRS, tt, LANES), lambda j, i: (i, 0, j, 0)),
                  pl.BlockSpec((nb, CONV_WIDTH // LANES, tt, LANES), lambda j, i: (i, 0, j, 0)),
                  pl.BlockSpec((nb, t, z.shape[2]), lambda j, i: (i, 0, 0)),
                  dft_blk, dft_blk, _resident(w_out.shape)],
        out_specs=(blk(d), blk(d)),
        compiler_params=_params(2),
        name="out_proj",
    )(x, mod, ret, conv, z, cos_m, sin_m, w_out)


def _ffn_kernel(*refs, d_model, hidden, fh, halo, final_norm):
    it = iter(refs)
    x_ref, h_ref = next(it), next(it)
    hp_ref = hn_ref = None
    if halo:
        hp_ref, hn_ref = next(it), next(it)
    mod_ref, wu_ref, cw_ref, wd_ref = (next(it) for _ in range(4))
    fg_ref = next(it) if final_norm else None
    o_ref, he_ref, act_ref = next(it), next(it), next(it)

    _, tt, d = x_ref.shape
    i = pl.program_id(1)
    he_ref[0:tt, :] = jnp.swapaxes(h_ref[0].reshape(HALO, tt // HALO, d), 0, 1).reshape(tt, d)
    if halo:
        rid = lax.broadcasted_iota(jnp.int32, (HALO, d), 0)
        before = jnp.where(i == 0, 0.0, hp_ref[0].astype(F32)[HALO - 1:HALO, :])
        after = jnp.where(i == pl.num_programs(1) - 1, 0.0, hn_ref[0].astype(F32)[0:1, :])
        edge = jnp.where(rid == 0, before, jnp.where(rid == 1, after, 0.0))
    else:
        edge = jnp.zeros((HALO, d), F32)
    he_ref[tt:tt + HALO, :] = edge.astype(BF16)

    def conv_rows(u, w):
        main = u[0:tt]
        first_prev = jnp.concatenate([u[tt:tt + 1], main[tt - HALO:tt - 1]], axis=0)
        last_next = jnp.concatenate([main[1:HALO], u[tt + 1:tt + 2]], axis=0)
        prev = jnp.concatenate([first_prev, main[0:tt - HALO]], axis=0)
        nxt = jnp.concatenate([main[HALO:tt], last_next], axis=0)
        return prev * w[0:1, :] + main * w[1:2, :] + nxt * w[2:3, :]

    for j in range(hidden // fh):
        vc = slice(j * fh, (j + 1) * fh)
        gc = slice(hidden + j * fh, hidden + (j + 1) * fh)
        he = he_ref[...]
        val = conv_rows(_dot(he, wu_ref[:, vc]), cw_ref[:, vc])
        gate = conv_rows(_dot(he, wu_ref[:, gc]), cw_ref[:, gc])
        act_ref[:, vc] = (val * (gate * jax.nn.sigmoid(gate))).astype(BF16)

    gate2 = mod_ref[0, :, 5 * d_model:6 * d_model]
    down = _dot(act_ref[...], wd_ref[...])
    y = x_ref[0] + gate2 * jnp.swapaxes(down.reshape(tt // HALO, HALO, d), 0, 1).reshape(tt, d)
    if final_norm:
        ms = jnp.mean(y * y, axis=-1, keepdims=True)
        y = (y * lax.rsqrt(ms + EPS)) * fg_ref[...]
    o_ref[0] = y


def _ffn(x, h, mod, w_up, conv_w, w_down, final_g, *, tt, fh):
    b, t, d = x.shape
    hidden = w_down.shape[0]
    assert hidden % fh == 0 and t % tt == 0 and tt % HALO == 0
    halo = tt < t
    shared_mod = mod.shape[0] == 1
    mod_map = (lambda bi, i: (0, 0, 0)) if shared_mod else (lambda bi, i: (bi, 0, 0))
    final_norm = final_g is not None
    tile = pl.BlockSpec((1, tt, d), lambda bi, i: (bi, i, 0))
    in_specs = [tile, tile]
    args = [x, h]
    if halo:
        per = tt // HALO
        last = t // HALO - 1
        in_specs += [
            pl.BlockSpec((1, HALO, d), lambda bi, i: (bi, jnp.maximum(i * per - 1, 0), 0)),
            pl.BlockSpec((1, HALO, d), lambda bi, i: (bi, jnp.minimum((i + 1) * per, last), 0)),
        ]
        args += [h, h]
    in_specs += [pl.BlockSpec((1, 1, mod.shape[2]), mod_map),
                 _resident(w_up.shape), _resident(conv_w.shape), _resident(w_down.shape)]
    args += [mod, w_up, conv_w, w_down]
    if final_norm:
        in_specs.append(_resident((1, d)))
        args.append(final_g.reshape(1, d))
    return pl.pallas_call(
        functools.partial(_ffn_kernel, d_model=d, hidden=hidden, fh=fh, halo=halo, final_norm=final_norm),
        out_shape=jax.ShapeDtypeStruct(x.shape, F32),
        grid=(b, t // tt),
        in_specs=in_specs,
        out_specs=tile,
        scratch_shapes=[pltpu.VMEM((tt + HALO, d), BF16), pltpu.VMEM((tt, hidden), BF16)],
        compiler_params=_params(2),
        name="ffn",
    )(*args)


def _rope_tables(t):
    rows = t // GRID_W
    row = jnp.repeat(jnp.arange(rows, dtype=F32), GRID_W)
    col = jnp.tile(jnp.arange(GRID_W, dtype=F32), rows)
    n_freq = HEAD_DIM // 4
    freq = ROPE_BASE ** (-jnp.arange(n_freq, dtype=F32) / n_freq)
    ang = jnp.concatenate([row[:, None] * freq, col[:, None] * freq], axis=-1)
    reps = LANES // ang.shape[1]
    cos = jnp.tile(jnp.cos(ang), (1, reps))
    sign = jnp.where((jnp.arange(LANES) % HEAD_DIM) < HEAD_DIM // 2, -1.0, 1.0).astype(F32)
    sin = jnp.tile(jnp.sin(ang), (1, reps)) * sign
    return cos, sin


def _decay_tables(decay_logit):
    ld = jax.nn.log_sigmoid(decay_logit.astype(F32))
    ld_f, ld_b = ld[0], ld[1]
    i = jnp.arange(CHUNK, dtype=F32)
    lane_f = jnp.repeat(ld_f, HEAD_DIM)[None, :]
    lane_b = jnp.repeat(ld_b, HEAD_DIM)[None, :]
    diff = i[:, None] - i[None, :]
    dmask = jnp.where(diff > 0, jnp.exp(ld_f[:, None, None] * jnp.maximum(diff, 0.0)),
                      jnp.where(diff < 0, jnp.exp(ld_b[:, None, None] * jnp.maximum(-diff, 0.0)), 2.0))
    def slabs(tab):
        return tab.reshape(CHUNK, HEAD_PAIRS, LANES).transpose(1, 0, 2)

    tables = {
        "dmask": dmask.reshape(HEAD_PAIRS, 2, CHUNK, CHUNK),
        "xi_f": jnp.exp(lane_f * (i[:, None] + 1.0)),
        "xi_b": jnp.exp(lane_b * (CHUNK - i[:, None])),
        "zeta_f": jnp.exp(lane_f * (CHUNK - 1.0 - i[:, None])),
        "zeta_b": jnp.exp(lane_b * i[:, None]),
        "cd_f": jnp.exp(lane_f * CHUNK),
        "cd_b": jnp.exp(lane_b * CHUNK),
    }
    for name in ("xi_f", "xi_b", "zeta_f", "zeta_b"):
        tables[name] = slabs(tables[name])
    return tables


def _position_dft(t):
    blk = GRID_W if t % GRID_W == 0 else 1
    r = jnp.arange(t, dtype=jnp.int32)[:, None]

    def table(cols):
        ang = ((r * cols[None, :]) % t).astype(F32) * (2.0 * np.pi / t)
        return jnp.cos(ang)[:, :, None], jnp.sin(ang)[:, :, None]

    ca, sa = table(jnp.arange(t // blk, dtype=jnp.int32) * blk)
    cb, sb = table(jnp.arange(blk, dtype=jnp.int32))
    cb, sb = cb.reshape(t, 1, blk), sb.reshape(t, 1, blk)
    cos = (ca * cb - sa * sb).reshape(t, t)
    sin = (sa * cb + ca * sb).reshape(t, t)
    return cos.astype(BF16), sin.astype(BF16)


def _channel_dft():
    n = FOURIER_GROUP_DIM
    idx = jnp.arange(n, dtype=jnp.int32)
    ang = ((idx[:, None] * idx[None, :]) % n).astype(F32) * (2.0 * np.pi / n)
    eye = jnp.eye(FOURIER_GROUPS, dtype=F32)
    return jnp.concatenate([jnp.kron(eye, jnp.cos(ang)), jnp.kron(eye, jnp.sin(ang))], axis=1).astype(BF16)


def kernel(x, c, ctx, c_ctx, w_mod, b_mod, w_in, ret_decay_logit, mix_conv_w, w_out,
           ffn_w_up, ffn_conv_w, ffn_w_down, final_norm_g):
    b, t, d = x.shape
    _, lc, _ = ctx.shape
    depth = w_mod.shape[0]

    rows = ((b + 1 + 7) // 8) * 8
    c_all = jnp.zeros((rows, d), F32).at[:b].set(c).at[b].set(c_ctx)
    mod_all = _modulation(c_all, w_mod, b_mod)

    rope = _rope_tables(t)
    dft_c = _channel_dft()
    dft_x = _position_dft(t)
    dft_ctx = _position_dft(lc)

    tt = min(t, 512)
    tt_big = min(t, 1024)
    nb_ctx = max(1, min(b, 1024 // lc))
    fh = 256

    for l in range(depth):
        last = l == depth - 1
        mod_x = mod_all[l, :b].reshape(b, 1, 6 * d)
        mod_c = mod_all[l, b:b + 1].reshape(1, 1, 6 * d)
        w_in_l = w_in[l].astype(BF16)
        w_out_l = w_out[l].astype(BF16)
        w_up_l = ffn_w_up[l].astype(BF16)
        w_down_l = ffn_w_down[l].astype(BF16)
        tables = _decay_tables(ret_decay_logit[l])

        if last:
            kz_c, v_c = _in_proj(ctx, mod_c, w_in_l, tables, None, dft_c, nb=nb_ctx, tt=lc, kv_only=True)
            st_f, st_b = _retention_states(kz_c, v_c, tables)
        else:
            *qkvgz_c, z_c, pc_c = _in_proj(ctx, mod_c, w_in_l, tables, None, dft_c, nb=nb_ctx, tt=lc)
            ret_c, conv_c, st_f, st_b = _mixers(*qkvgz_c, pc_c, tables, mix_conv_w[l], None, out_states=True)

        *qkvgz, z, pc = _in_proj(x, mod_x, w_in_l, tables, rope, dft_c, nb=1, tt=tt_big)
        ret, conv = _mixers(*qkvgz, pc, tables, mix_conv_w[l], (st_f, st_b), out_states=False)
        x, h = _out_proj(x, mod_x, ret, conv, z, *dft_x, w_out_l, nb=1, tt=tt_big)
        x = _ffn(x, h, mod_x, w_up_l, ffn_conv_w[l], w_down_l, final_norm_g if last else None, tt=tt_big, fh=fh)

        if not last:
            ctx, h_c = _out_proj(ctx, mod_c, ret_c, conv_c, z_c, *dft_ctx, w_out_l, nb=nb_ctx, tt=lc)
            ctx = _ffn(ctx, h_c, mod_c, w_up_l, ffn_conv_w[l], w_down_l, None, tt=lc, fh=fh)
    return x
```

```python
import functools

import jax
import jax.numpy as jnp
import numpy as np
from jax import lax
from jax.experimental import pallas as pl
from jax.experimental.pallas import tpu as pltpu

F32 = jnp.float32
BF16 = jnp.bfloat16

GRID_W = 64
RET_HEADS = 8
HEAD_DIM = 64
RET_WIDTH = RET_HEADS * HEAD_DIM
FOURIER_GROUPS = 4
FOURIER_GROUP_DIM = 64
FOURIER_WIDTH = FOURIER_GROUPS * FOURIER_GROUP_DIM
CONV_WIDTH = 256
ROPE_BASE = 10000.0
EPS = 1e-6
CHUNK = 128

LANES = 128
BF16_SUBLANES = 16
V7X_VMEM_LIMIT_BYTES = 60000 * 1024

HEAD_PAIRS = RET_WIDTH // LANES
HALO = BF16_SUBLANES
OUT_PROJ_ROWS = 512
PERM = 8

Q0 = 0
K0 = Q0 + RET_WIDTH
V0 = K0 + RET_WIDTH
G0 = V0 + RET_WIDTH
F0 = G0 + RET_WIDTH
CB0 = F0 + FOURIER_WIDTH
IN_WIDTH = CB0 + 3 * CONV_WIDTH


def _params(n_grid_dims, vmem=V7X_VMEM_LIMIT_BYTES):
    return pltpu.CompilerParams(
        dimension_semantics=("arbitrary",) * n_grid_dims,
        vmem_limit_bytes=vmem,
    )


def _dot(a, b):
    return jnp.dot(a, b, preferred_element_type=F32)


def _mod_norm(x, shift, scale):
    ms = jnp.mean(x * x, axis=-1, keepdims=True)
    return (x * lax.rsqrt(ms + EPS)) * (1.0 + scale) + shift


def _resident(shape):
    return pl.BlockSpec(shape, lambda *_: (0,) * len(shape), pipeline_mode=pl.Buffered(1))


def _mod_kernel(c_ref, w_ref, b_ref, o_ref):
    c = c_ref[...]
    s = c * jax.nn.sigmoid(c)
    o_ref[0] = jnp.dot(s, w_ref[0], preferred_element_type=F32,
                       precision=lax.Precision.HIGHEST) + b_ref[0]


def _modulation(c_all, w_mod, b_mod):
    depth, d, n = w_mod.shape
    rows = c_all.shape[0]
    tn = 1024
    return pl.pallas_call(
        _mod_kernel,
        out_shape=jax.ShapeDtypeStruct((depth, rows, n), F32),
        grid=(depth, n // tn),
        in_specs=[
            pl.BlockSpec((rows, d), lambda l, j: (0, 0)),
            pl.BlockSpec((1, d, tn), lambda l, j: (l, 0, j)),
            pl.BlockSpec((1, 1, tn), lambda l, j: (l, 0, j)),
        ],
        out_specs=pl.BlockSpec((1, rows, tn), lambda l, j: (l, 0, j)),
        compiler_params=_params(2),
        name="modulation",
    )(c_all, w_mod, b_mod.reshape(depth, 1, n))


def _rope_block(t, cos, sin, first_half):
    lo = pltpu.roll(t, LANES - HEAD_DIM // 2, 1)
    hi = pltpu.roll(t, HEAD_DIM // 2, 1)
    return t * cos + jnp.where(first_half, lo, hi) * sin


def _in_proj_kernel(*refs, d_model, use_rope, kv_only):
    it = iter(refs)
    x_ref, mod_ref, w_ref, zf_ref, zb_ref = (next(it) for _ in range(5))
    cos_ref = sin_ref = dft_ref = None
    if use_rope:
        cos_ref, sin_ref = next(it), next(it)
    if not kv_only:
        dft_ref = next(it)
    outs = list(it)

    nb, tt, d = x_ref.shape
    m = nb * tt
    x = x_ref[...].reshape(m, d)
    shift = mod_ref[0, :, 0:d_model]
    scale = mod_ref[0, :, d_model:2 * d_model]
    h = _mod_norm(x, shift, scale).astype(BF16)

    def put(ref, col, val):
        ref[:, :, col:col + val.shape[1]] = val.astype(ref.dtype).reshape(nb, tt, val.shape[1])

    def put_block(ref, j, val):
        ref[:, j, :, :] = val.astype(ref.dtype).reshape(nb, tt, LANES)

    if use_rope:
        cos = cos_ref[...]
        sin = sin_ref[...]
        lane = lax.broadcasted_iota(jnp.int32, (m, LANES), 1)
        first_half = (lane % HEAD_DIM) < (HEAD_DIM // 2)

    def rotary(t, j):
        tb = t[:, j * LANES:(j + 1) * LANES]
        return _rope_block(tb, cos, sin, first_half) if use_rope else tb

    def chunk_tiled(ref, j):
        return jnp.concatenate([ref[j]] * (m // CHUNK), axis=0)

    kz_ref, v_ref = (outs[0], outs[1]) if kv_only else (outs[4], outs[2])
    k = _dot(h, w_ref[:, K0:V0])
    v = _dot(h, w_ref[:, V0:G0])
    for j in range(HEAD_PAIRS):
        kb = rotary(k, j) * (HEAD_DIM ** -0.5)
        if not kv_only:
            put_block(outs[1], j, kb)
        put_block(kz_ref, j, kb * chunk_tiled(zf_ref, j))
        put_block(kz_ref, HEAD_PAIRS + j, kb * chunk_tiled(zb_ref, j))
        put_block(v_ref, j, v[:, j * LANES:(j + 1) * LANES])
    if kv_only:
        return

    q_ref, _, _, g_ref, _, z_ref, pc_ref = outs
    q = _dot(h, w_ref[:, Q0:K0])
    g = _dot(h, w_ref[:, G0:F0])
    g = g * jax.nn.sigmoid(g)
    for j in range(HEAD_PAIRS):
        put_block(q_ref, j, rotary(q, j))
        put_block(g_ref, j, g[:, j * LANES:(j + 1) * LANES])
    fpc = _dot(h, w_ref[:, F0:IN_WIDTH])
    f = fpc[:, 0:FOURIER_WIDTH].astype(BF16)
    put(z_ref, 0, _dot(f, dft_ref[...]))
    cb = fpc[:, CB0 - F0:CB0 - F0 + CONV_WIDTH]
    gp = fpc[:, CB0 - F0 + CONV_WIDTH:CB0 - F0 + 2 * CONV_WIDTH] * fpc[:, CB0 - F0 + 2 * CONV_WIDTH:]
    for j in range(CONV_WIDTH // LANES):
        put_block(pc_ref, j, cb[:, j * LANES:(j + 1) * LANES])
        put_block(pc_ref, CONV_WIDTH // LANES + j, gp[:, j * LANES:(j + 1) * LANES])


def _in_proj(x, mod, w_in, tables, rope, dft_c, *, nb, tt, kv_only=False):
    b, t, d = x.shape
    use_rope = rope is not None
    shared_mod = mod.shape[0] == 1
    assert nb == 1 or (shared_mod and not use_rope)
    assert tt % CHUNK == 0
    mod_map = (lambda i, j: (0, 0, 0)) if shared_mod else (lambda i, j: (i, 0, 0))
    in_specs = [
        pl.BlockSpec((nb, tt, d), lambda i, j: (i, j, 0)),
        pl.BlockSpec((1, 1, mod.shape[2]), mod_map),
        _resident(w_in.shape),
        _resident(tables["zeta_f"].shape),
        _resident(tables["zeta_b"].shape),
    ]
    args = [x, mod, w_in, tables["zeta_f"], tables["zeta_b"]]
    if use_rope:
        in_specs += [pl.BlockSpec((tt, LANES), lambda i, j: (j, 0))] * 2
        args += list(rope)
    if kv_only:
        widths = (2 * RET_WIDTH, RET_WIDTH)
        blocked = (True, True)
    else:
        in_specs.append(_resident(dft_c.shape))
        args.append(dft_c)
        widths = (RET_WIDTH,) * 4 + (2 * RET_WIDTH, 2 * FOURIER_WIDTH, 2 * CONV_WIDTH)
        blocked = (True,) * 5 + (False, True)
    out_shape = tuple(jax.ShapeDtypeStruct((b, w // LANES, t, LANES) if blk else (b, t, w), BF16)
                      for w, blk in zip(widths, blocked))
    out_specs = tuple(pl.BlockSpec((nb, w // LANES, tt, LANES), lambda i, j: (i, 0, j, 0)) if blk
                      else pl.BlockSpec((nb, tt, w), lambda i, j: (i, j, 0))
                      for w, blk in zip(widths, blocked))
    return pl.pallas_call(
        functools.partial(_in_proj_kernel, d_model=d, use_rope=use_rope, kv_only=kv_only),
        out_shape=out_shape,
        grid=(b // nb, t // tt),
        in_specs=in_specs,
        out_specs=out_specs,
        compiler_params=_params(2),
        name="in_proj_kv" if kv_only else "in_proj",
    )(*args)


def _block_diag_mask():
    row = lax.broadcasted_iota(jnp.int32, (LANES, LANES), 0)
    col = lax.broadcasted_iota(jnp.int32, (LANES, LANES), 1)
    return (row < HEAD_DIM) == (col < HEAD_DIM)


def _state_sweep(kz_ref, v_ref, cdf_ref, cdb_ref, sf_ref, sb_ref, s_all, n_chunks):
    diag = _block_diag_mask()

    def update(state_ref, p, kz, v, decay):
        u = lax.dot_general(kz, v, (((0,), (0,)), ((), ())), preferred_element_type=F32)
        state_ref[p] = decay * state_ref[p] + jnp.where(diag, u, 0.0)

    def body(i, carry):
        rf = pl.multiple_of(i * CHUNK, CHUNK)
        nb_ = n_chunks - 1 - i
        rb = pl.multiple_of(nb_ * CHUNK, CHUNK)
        for p in range(HEAD_PAIRS):
            lanes = slice(p * LANES, (p + 1) * LANES)
            if s_all is not None:
                s_all[i, p, 0] = sf_ref[p].astype(BF16)
                s_all[nb_, p, 1] = sb_ref[p].astype(BF16)
            update(sf_ref, p, kz_ref[0, p, pl.ds(rf, CHUNK), :], v_ref[0, p, pl.ds(rf, CHUNK), :],
                   cdf_ref[:, lanes])
            update(sb_ref, p, kz_ref[0, HEAD_PAIRS + p, pl.ds(rb, CHUNK), :], v_ref[0, p, pl.ds(rb, CHUNK), :],
                   cdb_ref[:, lanes])
        return carry

    lax.fori_loop(0, n_chunks, body, 0, unroll=8 if n_chunks % 8 == 0 else (2 if n_chunks % 2 == 0 else 1))


def _retention_states_kernel(kz_ref, v_ref, cdf_ref, cdb_ref, sf_out, sb_out, *, t):
    sf_out[...] = jnp.zeros(sf_out.shape, F32)
    sb_out[...] = jnp.zeros(sb_out.shape, F32)
    _state_sweep(kz_ref, v_ref, cdf_ref, cdb_ref, sf_out.at[0], sb_out.at[0], None, t // CHUNK)


def _blocked_spec(a):
    return pl.BlockSpec((1,) + a.shape[1:], lambda i: (i, 0, 0, 0))


def _retention_states(kz, v, tables):
    b, _, t, _ = v.shape
    state = jax.ShapeDtypeStruct((b, HEAD_PAIRS, LANES, LANES), F32)
    ins = [tables["cd_f"], tables["cd_b"]]
    return pl.pallas_call(
        functools.partial(_retention_states_kernel, t=t),
        out_shape=(state, state),
        grid=(b,),
        in_specs=[_blocked_spec(kz), _blocked_spec(v)] + [_resident(a.shape) for a in ins],
        out_specs=(pl.BlockSpec((1, HEAD_PAIRS, LANES, LANES), lambda i: (i, 0, 0, 0)),) * 2,
        compiler_params=_params(1),
        name="retention_states",
    )(kz, v, *ins)


def _mixers_kernel(*refs, t, has_init, out_states):
    it = iter(refs)
    q_ref, k_ref, v_ref, g_ref, kz_ref, pc_ref = (next(it) for _ in range(6))
    dmask_ref, xif_ref, xib_ref, cdf_ref, cdb_ref, cw_ref = (next(it) for _ in range(6))
    if has_init:
        if_ref, ib_ref = next(it), next(it)
    ret_ref, conv_ref = next(it), next(it)
    if out_states:
        sf_out, sb_out = next(it), next(it)
    sf_ref, sb_ref, s_all = next(it), next(it), next(it)

    n_chunks = t // CHUNK
    if has_init:
        sf_ref[...] = if_ref[0]
        sb_ref[...] = ib_ref[0]
    else:
        sf_ref[...] = jnp.zeros(sf_ref.shape, F32)
        sb_ref[...] = jnp.zeros(sb_ref.shape, F32)

    _state_sweep(kz_ref, v_ref, cdf_ref, cdb_ref, sf_ref, sb_ref, s_all, n_chunks)
    if out_states:
        sf_out[0] = sf_ref[...]
        sb_out[0] = sb_ref[...]

    lane = lax.broadcasted_iota(jnp.int32, (1, LANES), 1)
    head0 = lane < HEAD_DIM
    m0 = jnp.where(head0, 1.0, 0.0).astype(BF16)
    m1 = jnp.where(head0, 0.0, 1.0).astype(BF16)
    w = cw_ref[...]
    a0 = jnp.where(head0, 1.0 / HEAD_DIM, 0.0)
    a1 = jnp.where(head0, 0.0, 1.0 / HEAD_DIM)

    def out_chunk(n, carry):
        r0 = pl.multiple_of(n * CHUNK, CHUNK)
        rows = pl.ds(r0, CHUNK)
        for p in range(HEAD_PAIRS):
            lanes = slice(p * LANES, (p + 1) * LANES)
            q = q_ref[0, p, rows, :]
            k = k_ref[0, p, rows, :]
            v = v_ref[0, p, rows, :]
            kk = jnp.concatenate([k * m0, k * m1], axis=0)
            vv = jnp.concatenate([v * m0, v * m1], axis=0)
            s = lax.dot_general(q, kk, (((1,), (1,)), ((), ())), preferred_element_type=F32)
            dmask = jnp.concatenate([dmask_ref[p, 0], dmask_ref[p, 1]], axis=1)
            pr = (s * dmask).astype(BF16)
            o = _dot(pr, vv)
            c = _dot(q, jnp.concatenate([s_all[n, p, 0], s_all[n, p, 1]], axis=1))
            o = o + c[:, 0:LANES] * xif_ref[p] + c[:, LANES:2 * LANES] * xib_ref[p]
            o2 = o * o + EPS
            r0n = lax.rsqrt(jnp.sum(o2 * a0, axis=-1, keepdims=True))
            r1n = lax.rsqrt(jnp.sum(o2 * a1, axis=-1, keepdims=True))
            o = o * jnp.where(head0, r0n, r1n)
            ret_ref[0, p, rows, :] = (o * g_ref[0, p, rows, :].astype(F32)).astype(ret_ref.dtype)

        def conv_rows(first, start, size):
            return jnp.concatenate([pc_ref[0, first + j, pl.ds(start, size), :]
                                    for j in range(CONV_WIDTH // LANES)], axis=1).astype(F32)

        def gate_prod(start, size):
            return conv_rows(CONV_WIDTH // LANES, start, size)

        lo = pl.multiple_of(jnp.maximum(r0 - HALO, 0), HALO)
        hi = pl.multiple_of(jnp.minimum(r0 + CHUNK, t - HALO), HALO)
        before = jnp.where(n == 0, 0.0, gate_prod(lo, HALO))
        after = jnp.where(n == n_chunks - 1, 0.0, gate_prod(hi, HALO))
        mid = gate_prod(r0, CHUNK)
        ext = jnp.concatenate([before, mid, after], axis=0)
        acc = (ext[HALO - 1:HALO - 1 + CHUNK] * w[0:1, :] + mid * w[1:2, :]
               + ext[HALO + 1:HALO + 1 + CHUNK] * w[2:3, :])
        conv = (conv_rows(0, r0, CHUNK) * acc).astype(conv_ref.dtype)
        for j in range(CONV_WIDTH // LANES):
            conv_ref[0, j, rows, :] = conv[:, j * LANES:(j + 1) * LANES]
        return carry

    lax.fori_loop(0, n_chunks, out_chunk, 0, unroll=8 if n_chunks % 8 == 0 else 2)


def _mixers(q, k, v, g, kz, pc, tables, conv_w, init, *, out_states):
    b, _, t, _ = q.shape
    has_init = init is not None
    n_chunks = t // CHUNK
    tabs = [tables[k] for k in ("dmask", "xi_f", "xi_b", "cd_f", "cd_b")] + [conv_w]
    args = [q, k, v, g, kz, pc]
    in_specs = [_blocked_spec(a) for a in args] + [_resident(a.shape) for a in tabs]
    args = args + tabs
    state_spec = pl.BlockSpec((1, HEAD_PAIRS, LANES, LANES), lambda i: (i, 0, 0, 0))
    if has_init:
        in_specs += [state_spec, state_spec]
        args += list(init)
    out_shape = [jax.ShapeDtypeStruct((b, HEAD_PAIRS, t, LANES), BF16),
                 jax.ShapeDtypeStruct((b, CONV_WIDTH // LANES, t, LANES), BF16)]
    out_specs = [pl.BlockSpec((1, HEAD_PAIRS, t, LANES), lambda i: (i, 0, 0, 0)),
                 pl.BlockSpec((1, CONV_WIDTH // LANES, t, LANES), lambda i: (i, 0, 0, 0))]
    if out_states:
        out_shape += [jax.ShapeDtypeStruct((b, HEAD_PAIRS, LANES, LANES), F32)] * 2
        out_specs += [state_spec, state_spec]
    return pl.pallas_call(
        functools.partial(_mixers_kernel, t=t, has_init=has_init, out_states=out_states),
        out_shape=tuple(out_shape),
        grid=(b,),
        in_specs=in_specs,
        out_specs=tuple(out_specs),
        scratch_shapes=[
            pltpu.VMEM((HEAD_PAIRS, LANES, LANES), F32),
            pltpu.VMEM((HEAD_PAIRS, LANES, LANES), F32),
            pltpu.VMEM((n_chunks, HEAD_PAIRS, 2, LANES, LANES), BF16),
        ],
        compiler_params=_params(1),
        name="mixers",
    )(*args)


def _out_proj_kernel(x_ref, mod_ref, ret_ref, conv_ref, z_ref, cm_ref, sm_ref, w_ref, o_ref, h_ref,
                     *, d_model, dft_scale):
    nb, tt, d = x_ref.shape
    gate = mod_ref[0, :, 2 * d_model:3 * d_model]
    shift = mod_ref[0, :, 3 * d_model:4 * d_model]
    scale = mod_ref[0, :, 4 * d_model:5 * d_model]
    rb = min(tt, OUT_PROJ_ROWS)
    for n in range(nb):
        zr = z_ref[n, :, 0:FOURIER_WIDTH]
        zi = z_ref[n, :, FOURIER_WIDTH:2 * FOURIER_WIDTH]
        for r in range(tt // rb):
            rows = slice(r * rb, (r + 1) * rb)
            four = ((_dot(cm_ref[rows, :], zr) - _dot(sm_ref[rows, :], zi)) * dft_scale).astype(BF16)
            ret = jnp.concatenate([ret_ref[n, p, rows, :] for p in range(HEAD_PAIRS)], axis=1)
            acc = _dot(ret, w_ref[0:RET_WIDTH, :])
            acc = acc + _dot(four, w_ref[RET_WIDTH:RET_WIDTH + FOURIER_WIDTH, :])
            conv = jnp.concatenate([conv_ref[n, j, rows, :] for j in range(CONV_WIDTH // LANES)], axis=1)
            acc = acc + _dot(conv, w_ref[RET_WIDTH + FOURIER_WIDTH:, :])
            y = x_ref[n, rows, :] + gate * acc
            o_ref[n, rows, :] = y
            h_ref[n, rows, :] = _mod_norm(y, shift, scale).astype(h_ref.dtype)


def _out_proj(x, mod, ret, conv, z, cos_m, sin_m, w_out, *, nb, tt):
    b, t, d = x.shape
    shared_mod = mod.shape[0] == 1
    assert nb == 1 or (shared_mod and tt == t)
    mod_map = (lambda j, i: (0, 0, 0)) if shared_mod else (lambda j, i: (i, 0, 0))
    blk = lambda w: pl.BlockSpec((nb, tt, w), lambda j, i: (i, j, 0))
    dft_blk = pl.BlockSpec((tt, t), lambda j, i: (j, 0), pipeline_mode=pl.Buffered(1))
    dft_scale = float(1.0 / np.sqrt(float(t) * FOURIER_GROUP_DIM))
    return pl.pallas_call(
        functools.partial(_out_proj_kernel, d_model=d, dft_scale=dft_scale),
        out_shape=(jax.ShapeDtypeStruct(x.shape, F32), jax.ShapeDtypeStruct(x.shape, BF16)),
        grid=(t // tt, b // nb),
        in_specs=[blk(d), pl.BlockSpec((1, 1, mod.shape[2]), mod_map),
                  pl.BlockSpec((nb, HEAD_PAIRS, tt, LANES), lambda j, i: (i, 0, j, 0)),
                  pl.BlockSpec((nb, CONV_WIDTH // LANES, tt, LANES), lambda j, i: (i, 0, j, 0)),
                  pl.BlockSpec((nb, t, z.shape[2]), lambda j, i: (i, 0, 0)),
                  dft_blk, dft_blk, _resident(w_out.shape)],
        out_specs=(blk(d), blk(d)),
        compiler_params=_params(2),
        name="out_proj",
    )(x, mod, ret, conv, z, cos_m, sin_m, w_out)


def _ffn_kernel(*refs, d_model, hidden, fh, halo, final_norm):
    it = iter(refs)
    x_ref, h_ref = next(it), next(it)
    hp_ref = hn_ref = None
    if halo:
        hp_ref, hn_ref = next(it), next(it)
    mod_ref, wu_ref, cw_ref, wd_ref = (next(it) for _ in range(4))
    fg_ref = next(it) if final_norm else None
    o_ref, he_ref, act_ref = next(it), next(it), next(it)

    _, tt, d = x_ref.shape
    i = pl.program_id(1)
    he_ref[0:tt, :] = jnp.swapaxes(h_ref[0].reshape(PERM, tt // PERM, d), 0, 1).reshape(tt, d)
    if halo:
        rid = lax.broadcasted_iota(jnp.int32, (HALO, d), 0)
        before = jnp.where(i == 0, 0.0, hp_ref[0].astype(F32)[HALO - 1:HALO, :])
        after = jnp.where(i == pl.num_programs(1) - 1, 0.0, hn_ref[0].astype(F32)[0:1, :])
        edge = jnp.where(rid == 0, before, jnp.where(rid == 1, after, 0.0))
    else:
        edge = jnp.zeros((HALO, d), F32)
    he_ref[tt:tt + HALO, :] = edge.astype(BF16)

    def conv_rows(u, w):
        main = u[0:tt]
        first_prev = jnp.concatenate([u[tt:tt + 1], main[tt - PERM:tt - 1]], axis=0)
        last_next = jnp.concatenate([main[1:PERM], u[tt + 1:tt + 2]], axis=0)
        prev = jnp.concatenate([first_prev, main[0:tt - PERM]], axis=0)
        nxt = jnp.concatenate([main[PERM:tt], last_next], axis=0)
        return prev * w[0:1, :] + main * w[1:2, :] + nxt * w[2:3, :]

    for j in range(hidden // fh):
        vc = slice(j * fh, (j + 1) * fh)
        gc = slice(hidden + j * fh, hidden + (j + 1) * fh)
        he = he_ref[...]
        val = conv_rows(_dot(he, wu_ref[:, vc]), cw_ref[:, vc])
        gate = conv_rows(_dot(he, wu_ref[:, gc]), cw_ref[:, gc])
        act_ref[:, vc] = (val * (gate * jax.nn.sigmoid(gate))).astype(BF16)

    gate2 = mod_ref[0, :, 5 * d_model:6 * d_model]
    down = _dot(act_ref[...], wd_ref[...])
    y = x_ref[0] + gate2 * jnp.swapaxes(down.reshape(tt // PERM, PERM, d), 0, 1).reshape(tt, d)
    if final_norm:
        ms = jnp.mean(y * y, axis=-1, keepdims=True)
        y = (y * lax.rsqrt(ms + EPS)) * fg_ref[...]
    o_ref[0] = y


def _ffn(x, h, mod, w_up, conv_w, w_down, final_g, *, tt, fh):
    b, t, d = x.shape
    hidden = w_down.shape[0]
    assert hidden % fh == 0 and t % tt == 0 and tt % HALO == 0
    halo = tt < t
    shared_mod = mod.shape[0] == 1
    mod_map = (lambda bi, i: (0, 0, 0)) if shared_mod else (lambda bi, i: (bi, 0, 0))
    final_norm = final_g is not None
    tile = pl.BlockSpec((1, tt, d), lambda bi, i: (bi, i, 0))
    in_specs = [tile, tile]
    args = [x, h]
    if halo:
        per = tt // HALO
        last = t // HALO - 1
        in_specs += [
            pl.BlockSpec((1, HALO, d), lambda bi, i: (bi, jnp.maximum(i * per - 1, 0), 0)),
            pl.BlockSpec((1, HALO, d), lambda bi, i: (bi, jnp.minimum((i + 1) * per, last), 0)),
        ]
        args += [h, h]
    in_specs += [pl.BlockSpec((1, 1, mod.shape[2]), mod_map),
                 _resident(w_up.shape), _resident(conv_w.shape), _resident(w_down.shape)]
    args += [mod, w_up, conv_w, w_down]
    if final_norm:
        in_specs.append(_resident((1, d)))
        args.append(final_g.reshape(1, d))
    return pl.pallas_call(
        functools.partial(_ffn_kernel, d_model=d, hidden=hidden, fh=fh, halo=halo, final_norm=final_norm),
        out_shape=jax.ShapeDtypeStruct(x.shape, F32),
        grid=(b, t // tt),
        in_specs=in_specs,
        out_specs=tile,
        scratch_shapes=[pltpu.VMEM((tt + HALO, d), BF16), pltpu.VMEM((tt, hidden), BF16)],
        compiler_params=_params(2),
        name="ffn",
    )(*args)


def _rope_tables(t):
    rows = t // GRID_W
    row = jnp.repeat(jnp.arange(rows, dtype=F32), GRID_W)
    col = jnp.tile(jnp.arange(GRID_W, dtype=F32), rows)
    n_freq = HEAD_DIM // 4
    freq = ROPE_BASE ** (-jnp.arange(n_freq, dtype=F32) / n_freq)
    ang = jnp.concatenate([row[:, None] * freq, col[:, None] * freq], axis=-1)
    reps = LANES // ang.shape[1]
    cos = jnp.tile(jnp.cos(ang), (1, reps))
    sign = jnp.where((jnp.arange(LANES) % HEAD_DIM) < HEAD_DIM // 2, -1.0, 1.0).astype(F32)
    sin = jnp.tile(jnp.sin(ang), (1, reps)) * sign
    return cos, sin


def _decay_tables(decay_logit):
    ld = jax.nn.log_sigmoid(decay_logit.astype(F32))
    ld_f, ld_b = ld[0], ld[1]
    i = jnp.arange(CHUNK, dtype=F32)
    lane_f = jnp.repeat(ld_f, HEAD_DIM)[None, :]
    lane_b = jnp.repeat(ld_b, HEAD_DIM)[None, :]
    diff = i[:, None] - i[None, :]
    dmask = jnp.where(diff > 0, jnp.exp(ld_f[:, None, None] * jnp.maximum(diff, 0.0)),
                      jnp.where(diff < 0, jnp.exp(ld_b[:, None, None] * jnp.maximum(-diff, 0.0)), 2.0))
    def slabs(tab):
        return tab.reshape(CHUNK, HEAD_PAIRS, LANES).transpose(1, 0, 2)

    tables = {
        "dmask": dmask.reshape(HEAD_PAIRS, 2, CHUNK, CHUNK),
        "xi_f": jnp.exp(lane_f * (i[:, None] + 1.0)),
        "xi_b": jnp.exp(lane_b * (CHUNK - i[:, None])),
        "zeta_f": jnp.exp(lane_f * (CHUNK - 1.0 - i[:, None])),
        "zeta_b": jnp.exp(lane_b * i[:, None]),
        "cd_f": jnp.exp(lane_f * CHUNK),
        "cd_b": jnp.exp(lane_b * CHUNK),
    }
    for name in ("xi_f", "xi_b", "zeta_f", "zeta_b"):
        tables[name] = slabs(tables[name])
    return tables


def _position_dft(t):
    blk = GRID_W if t % GRID_W == 0 else 1
    r = jnp.arange(t, dtype=jnp.int32)[:, None]

    def table(cols):
        ang = ((r * cols[None, :]) % t).astype(F32) * (2.0 * np.pi / t)
        return jnp.cos(ang)[:, :, None], jnp.sin(ang)[:, :, None]

    ca, sa = table(jnp.arange(t // blk, dtype=jnp.int32) * blk)
    cb, sb = table(jnp.arange(blk, dtype=jnp.int32))
    cb, sb = cb.reshape(t, 1, blk), sb.reshape(t, 1, blk)
    cos = (ca * cb - sa * sb).reshape(t, t)
    sin = (sa * cb + ca * sb).reshape(t, t)
    return cos.astype(BF16), sin.astype(BF16)


def _channel_dft():
    n = FOURIER_GROUP_DIM
    idx = jnp.arange(n, dtype=jnp.int32)
    ang = ((idx[:, None] * idx[None, :]) % n).astype(F32) * (2.0 * np.pi / n)
    eye = jnp.eye(FOURIER_GROUPS, dtype=F32)
    return jnp.concatenate([jnp.kron(eye, jnp.cos(ang)), jnp.kron(eye, jnp.sin(ang))], axis=1).astype(BF16)


def kernel(x, c, ctx, c_ctx, w_mod, b_mod, w_in, ret_decay_logit, mix_conv_w, w_out,
           ffn_w_up, ffn_conv_w, ffn_w_down, final_norm_g):
    b, t, d = x.shape
    _, lc, _ = ctx.shape
    depth = w_mod.shape[0]

    rows = ((b + 1 + 7) // 8) * 8
    c_all = jnp.zeros((rows, d), F32).at[:b].set(c).at[b].set(c_ctx)
    mod_all = _modulation(c_all, w_mod, b_mod)

    rope = _rope_tables(t)
    dft_c = _channel_dft()
    dft_x = _position_dft(t)
    dft_ctx = _position_dft(lc)

    tt = min(t, 512)
    tt_big = min(t, 1024)
    nb_ctx = max(1, min(b, 1024 // lc))
    fh = 256

    for l in range(depth):
        last = l == depth - 1
        mod_x = mod_all[l, :b].reshape(b, 1, 6 * d)
        mod_c = mod_all[l, b:b + 1].reshape(1, 1, 6 * d)
        w_in_l = w_in[l].astype(BF16)
        w_out_l = w_out[l].astype(BF16)
        w_up_l = ffn_w_up[l].astype(BF16)
        w_down_l = ffn_w_down[l].astype(BF16)
        tables = _decay_tables(ret_decay_logit[l])

        if last:
            kz_c, v_c = _in_proj(ctx, mod_c, w_in_l, tables, None, dft_c, nb=nb_ctx, tt=lc, kv_only=True)
            st_f, st_b = _retention_states(kz_c, v_c, tables)
        else:
            *qkvgz_c, z_c, pc_c = _in_proj(ctx, mod_c, w_in_l, tables, None, dft_c, nb=nb_ctx, tt=lc)
            ret_c, conv_c, st_f, st_b = _mixers(*qkvgz_c, pc_c, tables, mix_conv_w[l], None, out_states=True)

        *qkvgz, z, pc = _in_proj(x, mod_x, w_in_l, tables, rope, dft_c, nb=1, tt=tt_big)
        ret, conv = _mixers(*qkvgz, pc, tables, mix_conv_w[l], (st_f, st_b), out_states=False)
        x, h = _out_proj(x, mod_x, ret, conv, z, *dft_x, w_out_l, nb=1, tt=tt_big)
        x = _ffn(x, h, mod_x, w_up_l, ffn_conv_w[l], w_down_l, final_norm_g if last else None, tt=tt_big, fh=fh)

        if not last:
            ctx, h_c = _out_proj(ctx, mod_c, ret_c, conv_c, z_c, *dft_ctx, w_out_l, nb=nb_ctx, tt=lc)
            ctx = _ffn(ctx, h_c, mod_c, w_up_l, ffn_conv_w[l], w_down_l, None, tt=lc, fh=fh)
    return x
```

```python
import functools

import jax
import jax.numpy as jnp
import numpy as np
from jax import lax
from jax.experimental import pallas as pl
from jax.experimental.pallas import tpu as pltpu

F32 = jnp.float32
BF16 = jnp.bfloat16

GRID_W = 64
RET_HEADS = 8
HEAD_DIM = 64
RET_WIDTH = RET_HEADS * HEAD_DIM
FOURIER_GROUPS = 4
FOURIER_GROUP_DIM = 64
FOURIER_WIDTH = FOURIER_GROUPS * FOURIER_GROUP_DIM
CONV_WIDTH = 256
ROPE_BASE = 10000.0
EPS = 1e-6
CHUNK = 128

LANES = 128
BF16_SUBLANES = 16
V7X_VMEM_LIMIT_BYTES = 60000 * 1024

HEAD_PAIRS = RET_WIDTH // LANES
HALO = BF16_SUBLANES
OUT_PROJ_ROWS = 512
PERM = 8

Q0 = 0
K0 = Q0 + RET_WIDTH
V0 = K0 + RET_WIDTH
G0 = V0 + RET_WIDTH
F0 = G0 + RET_WIDTH
CB0 = F0 + FOURIER_WIDTH
IN_WIDTH = CB0 + 3 * CONV_WIDTH


def _params(n_grid_dims, vmem=V7X_VMEM_LIMIT_BYTES):
    return pltpu.CompilerParams(
        dimension_semantics=("arbitrary",) * n_grid_dims,
        vmem_limit_bytes=vmem,
    )


def _dot(a, b):
    return jnp.dot(a, b, preferred_element_type=F32)


def _mod_norm(x, shift, scale):
    ms = jnp.mean(x * x, axis=-1, keepdims=True)
    return (x * lax.rsqrt(ms + EPS)) * (1.0 + scale) + shift


def _resident(shape):
    return pl.BlockSpec(shape, lambda *_: (0,) * len(shape), pipeline_mode=pl.Buffered(1))


def _mod_kernel(c_ref, w_ref, b_ref, o_ref):
    c = c_ref[...]
    s = c * jax.nn.sigmoid(c)
    o_ref[0] = jnp.dot(s, w_ref[0], preferred_element_type=F32,
                       precision=lax.Precision.HIGHEST) + b_ref[0]


def _modulation(c_all, w_mod, b_mod):
    depth, d, n = w_mod.shape
    rows = c_all.shape[0]
    tn = 1024
    return pl.pallas_call(
        _mod_kernel,
        out_shape=jax.ShapeDtypeStruct((depth, rows, n), F32),
        grid=(depth, n // tn),
        in_specs=[
            pl.BlockSpec((rows, d), lambda l, j: (0, 0)),
            pl.BlockSpec((1, d, tn), lambda l, j: (l, 0, j)),
            pl.BlockSpec((1, 1, tn), lambda l, j: (l, 0, j)),
        ],
        out_specs=pl.BlockSpec((1, rows, tn), lambda l, j: (l, 0, j)),
        compiler_params=_params(2),
        name="modulation",
    )(c_all, w_mod, b_mod.reshape(depth, 1, n))


def _rope_block(t, cos, sin, first_half):
    lo = pltpu.roll(t, LANES - HEAD_DIM // 2, 1)
    hi = pltpu.roll(t, HEAD_DIM // 2, 1)
    return t * cos + jnp.where(first_half, lo, hi) * sin


def _in_proj_kernel(*refs, d_model, use_rope, kv_only):
    it = iter(refs)
    x_ref, mod_ref, w_ref, zf_ref, zb_ref = (next(it) for _ in range(5))
    cos_ref = sin_ref = dft_ref = None
    if use_rope:
        cos_ref, sin_ref = next(it), next(it)
    if not kv_only:
        dft_ref = next(it)
    outs = list(it)

    nb, tt, d = x_ref.shape
    m = nb * tt
    x = x_ref[...].reshape(m, d)
    shift = mod_ref[0, :, 0:d_model]
    scale = mod_ref[0, :, d_model:2 * d_model]
    h = _mod_norm(x, shift, scale).astype(BF16)

    def put(ref, col, val):
        ref[:, :, col:col + val.shape[1]] = val.astype(ref.dtype).reshape(nb, tt, val.shape[1])

    def put_block(ref, j, val):
        ref[:, j, :, :] = val.astype(ref.dtype).reshape(nb, tt, LANES)

    if use_rope:
        cos = cos_ref[...]
        sin = sin_ref[...]
        lane = lax.broadcasted_iota(jnp.int32, (m, LANES), 1)
        first_half = (lane % HEAD_DIM) < (HEAD_DIM // 2)

    def rotary(t, j):
        tb = t[:, j * LANES:(j + 1) * LANES]
        return _rope_block(tb, cos, sin, first_half) if use_rope else tb

    def chunk_tiled(ref, j):
        return jnp.concatenate([ref[j]] * (m // CHUNK), axis=0)

    kz_ref, v_ref = (outs[0], outs[1]) if kv_only else (outs[4], outs[2])
    k = _dot(h, w_ref[:, K0:V0])
    v = _dot(h, w_ref[:, V0:G0])
    for j in range(HEAD_PAIRS):
        kb = rotary(k, j) * (HEAD_DIM ** -0.5)
        if not kv_only:
            put_block(outs[1], j, kb)
        put_block(kz_ref, j, kb * chunk_tiled(zf_ref, j))
        put_block(kz_ref, HEAD_PAIRS + j, kb * chunk_tiled(zb_ref, j))
        put_block(v_ref, j, v[:, j * LANES:(j + 1) * LANES])
    if kv_only:
        return

    q_ref, _, _, g_ref, _, z_ref, pc_ref = outs
    q = _dot(h, w_ref[:, Q0:K0])
    g = _dot(h, w_ref[:, G0:F0])
    g = g * jax.nn.sigmoid(g)
    for j in range(HEAD_PAIRS):
        put_block(q_ref, j, rotary(q, j))
        put_block(g_ref, j, g[:, j * LANES:(j + 1) * LANES])
    fpc = _dot(h, w_ref[:, F0:IN_WIDTH])
    f = fpc[:, 0:FOURIER_WIDTH].astype(BF16)
    put(z_ref, 0, _dot(f, dft_ref[...]))
    cb = fpc[:, CB0 - F0:CB0 - F0 + CONV_WIDTH]
    gp = fpc[:, CB0 - F0 + CONV_WIDTH:CB0 - F0 + 2 * CONV_WIDTH] * fpc[:, CB0 - F0 + 2 * CONV_WIDTH:]
    for j in range(CONV_WIDTH // LANES):
        put_block(pc_ref, j, cb[:, j * LANES:(j + 1) * LANES])
        put_block(pc_ref, CONV_WIDTH // LANES + j, gp[:, j * LANES:(j + 1) * LANES])


def _in_proj(x, mod, w_in, tables, rope, dft_c, *, nb, tt, kv_only=False):
    b, t, d = x.shape
    use_rope = rope is not None
    shared_mod = mod.shape[0] == 1
    assert nb == 1 or (shared_mod and not use_rope)
    assert tt % CHUNK == 0
    mod_map = (lambda i, j: (0, 0, 0)) if shared_mod else (lambda i, j: (i, 0, 0))
    in_specs = [
        pl.BlockSpec((nb, tt, d), lambda i, j: (i, j, 0)),
        pl.BlockSpec((1, 1, mod.shape[2]), mod_map),
        _resident(w_in.shape),
        _resident(tables["zeta_f"].shape),
        _resident(tables["zeta_b"].shape),
    ]
    args = [x, mod, w_in, tables["zeta_f"], tables["zeta_b"]]
    if use_rope:
        in_specs += [pl.BlockSpec((tt, LANES), lambda i, j: (j, 0))] * 2
        args += list(rope)
    if kv_only:
        widths = (2 * RET_WIDTH, RET_WIDTH)
        blocked = (True, True)
    else:
        in_specs.append(_resident(dft_c.shape))
        args.append(dft_c)
        widths = (RET_WIDTH,) * 4 + (2 * RET_WIDTH, 2 * FOURIER_WIDTH, 2 * CONV_WIDTH)
        blocked = (True,) * 5 + (False, True)
    out_shape = tuple(jax.ShapeDtypeStruct((b, w // LANES, t, LANES) if blk else (b, t, w), BF16)
                      for w, blk in zip(widths, blocked))
    out_specs = tuple(pl.BlockSpec((nb, w // LANES, tt, LANES), lambda i, j: (i, 0, j, 0)) if blk
                      else pl.BlockSpec((nb, tt, w), lambda i, j: (i, j, 0))
                      for w, blk in zip(widths, blocked))
    return pl.pallas_call(
        functools.partial(_in_proj_kernel, d_model=d, use_rope=use_rope, kv_only=kv_only),
        out_shape=out_shape,
        grid=(b // nb, t // tt),
        in_specs=in_specs,
        out_specs=out_specs,
        compiler_params=_params(2),
        name="in_proj_kv" if kv_only else "in_proj",
    )(*args)


def _block_diag_mask():
    row = lax.broadcasted_iota(jnp.int32, (LANES, LANES), 0)
    col = lax.broadcasted_iota(jnp.int32, (LANES, LANES), 1)
    return (row < HEAD_DIM) == (col < HEAD_DIM)


def _state_sweep(kz_ref, v_ref, cdf_ref, cdb_ref, sf_ref, sb_ref, s_all, n_chunks):
    diag = _block_diag_mask()

    def update(state_ref, p, kz, v, decay):
        u = lax.dot_general(kz, v, (((0,), (0,)), ((), ())), preferred_element_type=F32)
        state_ref[p] = decay * state_ref[p] + jnp.where(diag, u, 0.0)

    def body(i, carry):
        rf = pl.multiple_of(i * CHUNK, CHUNK)
        nb_ = n_chunks - 1 - i
        rb = pl.multiple_of(nb_ * CHUNK, CHUNK)
        for p in range(HEAD_PAIRS):
            lanes = slice(p * LANES, (p + 1) * LANES)
            if s_all is not None:
                s_all[i, p, 0] = sf_ref[p].astype(BF16)
                s_all[nb_, p, 1] = sb_ref[p].astype(BF16)
            update(sf_ref, p, kz_ref[0, p, pl.ds(rf, CHUNK), :], v_ref[0, p, pl.ds(rf, CHUNK), :],
                   cdf_ref[:, lanes])
            update(sb_ref, p, kz_ref[0, HEAD_PAIRS + p, pl.ds(rb, CHUNK), :], v_ref[0, p, pl.ds(rb, CHUNK), :],
                   cdb_ref[:, lanes])
        return carry

    lax.fori_loop(0, n_chunks, body, 0, unroll=8 if n_chunks % 8 == 0 else (2 if n_chunks % 2 == 0 else 1))


def _retention_states_kernel(kz_ref, v_ref, cdf_ref, cdb_ref, sf_out, sb_out, *, t):
    sf_out[...] = jnp.zeros(sf_out.shape, F32)
    sb_out[...] = jnp.zeros(sb_out.shape, F32)
    _state_sweep(kz_ref, v_ref, cdf_ref, cdb_ref, sf_out.at[0], sb_out.at[0], None, t // CHUNK)


def _blocked_spec(a):
    return pl.BlockSpec((1,) + a.shape[1:], lambda i: (i, 0, 0, 0))


def _retention_states(kz, v, tables):
    b, _, t, _ = v.shape
    state = jax.ShapeDtypeStruct((b, HEAD_PAIRS, LANES, LANES), F32)
    ins = [tables["cd_f"], tables["cd_b"]]
    return pl.pallas_call(
        functools.partial(_retention_states_kernel, t=t),
        out_shape=(state, state),
        grid=(b,),
        in_specs=[_blocked_spec(kz), _blocked_spec(v)] + [_resident(a.shape) for a in ins],
        out_specs=(pl.BlockSpec((1, HEAD_PAIRS, LANES, LANES), lambda i: (i, 0, 0, 0)),) * 2,
        compiler_params=_params(1),
        name="retention_states",
    )(kz, v, *ins)


def _mixers_kernel(*refs, t, has_init, out_states):
    it = iter(refs)
    q_ref, k_ref, v_ref, g_ref, kz_ref, pc_ref = (next(it) for _ in range(6))
    dmask_ref, xif_ref, xib_ref, cdf_ref, cdb_ref, cw_ref = (next(it) for _ in range(6))
    if has_init:
        if_ref, ib_ref = next(it), next(it)
    ret_ref, conv_ref = next(it), next(it)
    if out_states:
        sf_out, sb_out = next(it), next(it)
    sf_ref, sb_ref, s_all = next(it), next(it), next(it)

    n_chunks = t // CHUNK
    if has_init:
        sf_ref[...] = if_ref[0]
        sb_ref[...] = ib_ref[0]
    else:
        sf_ref[...] = jnp.zeros(sf_ref.shape, F32)
        sb_ref[...] = jnp.zeros(sb_ref.shape, F32)

    _state_sweep(kz_ref, v_ref, cdf_ref, cdb_ref, sf_ref, sb_ref, s_all, n_chunks)
    if out_states:
        sf_out[0] = sf_ref[...]
        sb_out[0] = sb_ref[...]

    lane = lax.broadcasted_iota(jnp.int32, (1, LANES), 1)
    head0 = lane < HEAD_DIM
    m0 = jnp.where(head0, 1.0, 0.0).astype(BF16)
    m1 = jnp.where(head0, 0.0, 1.0).astype(BF16)
    w = cw_ref[...]
    a0 = jnp.where(head0, 1.0 / HEAD_DIM, 0.0)
    a1 = jnp.where(head0, 0.0, 1.0 / HEAD_DIM)

    def out_chunk(n, carry):
        r0 = pl.multiple_of(n * CHUNK, CHUNK)
        rows = pl.ds(r0, CHUNK)
        for p in range(HEAD_PAIRS):
            lanes = slice(p * LANES, (p + 1) * LANES)
            q = q_ref[0, p, rows, :]
            k = k_ref[0, p, rows, :]
            v = v_ref[0, p, rows, :]
            kk = jnp.concatenate([k * m0, k * m1], axis=0)
            vv = jnp.concatenate([v * m0, v * m1], axis=0)
            s = lax.dot_general(q, kk, (((1,), (1,)), ((), ())), preferred_element_type=F32)
            dmask = jnp.concatenate([dmask_ref[p, 0], dmask_ref[p, 1]], axis=1)
            pr = (s * dmask).astype(BF16)
            o = _dot(pr, vv)
            c = _dot(q, jnp.concatenate([s_all[n, p, 0], s_all[n, p, 1]], axis=1))
            o = o + c[:, 0:LANES] * xif_ref[p] + c[:, LANES:2 * LANES] * xib_ref[p]
            o2 = o * o + EPS
            r0n = lax.rsqrt(jnp.sum(o2 * a0, axis=-1, keepdims=True))
            r1n = lax.rsqrt(jnp.sum(o2 * a1, axis=-1, keepdims=True))
            o = o * jnp.where(head0, r0n, r1n)
            ret_ref[0, p, rows, :] = (o * g_ref[0, p, rows, :].astype(F32)).astype(ret_ref.dtype)

        def conv_rows(first, start, size):
            return jnp.concatenate([pc_ref[0, first + j, pl.ds(start, size), :]
                                    for j in range(CONV_WIDTH // LANES)], axis=1).astype(F32)

        def gate_prod(start, size):
            return conv_rows(CONV_WIDTH // LANES, start, size)

        lo = pl.multiple_of(jnp.maximum(r0 - HALO, 0), HALO)
        hi = pl.multiple_of(jnp.minimum(r0 + CHUNK, t - HALO), HALO)
        before = jnp.where(n == 0, 0.0, gate_prod(lo, HALO))
        after = jnp.where(n == n_chunks - 1, 0.0, gate_prod(hi, HALO))
        mid = gate_prod(r0, CHUNK)
        ext = jnp.concatenate([before, mid, after], axis=0)
        acc = (ext[HALO - 1:HALO - 1 + CHUNK] * w[0:1, :] + mid * w[1:2, :]
               + ext[HALO + 1:HALO + 1 + CHUNK] * w[2:3, :])
        conv = (conv_rows(0, r0, CHUNK) * acc).astype(conv_ref.dtype)
        for j in range(CONV_WIDTH // LANES):
            conv_ref[0, j, rows, :] = conv[:, j * LANES:(j + 1) * LANES]
        return carry

    lax.fori_loop(0, n_chunks, out_chunk, 0, unroll=8 if n_chunks % 8 == 0 else 2)


def _mixers(q, k, v, g, kz, pc, tables, conv_w, init, *, out_states):
    b, _, t, _ = q.shape
    has_init = init is not None
    n_chunks = t // CHUNK
    tabs = [tables[k] for k in ("dmask", "xi_f", "xi_b", "cd_f", "cd_b")] + [conv_w]
    args = [q, k, v, g, kz, pc]
    in_specs = [_blocked_spec(a) for a in args] + [_resident(a.shape) for a in tabs]
    args = args + tabs
    state_spec = pl.BlockSpec((1, HEAD_PAIRS, LANES, LANES), lambda i: (i, 0, 0, 0))
    if has_init:
        in_specs += [state_spec, state_spec]
        args += list(init)
    out_shape = [jax.ShapeDtypeStruct((b, HEAD_PAIRS, t, LANES), BF16),
                 jax.ShapeDtypeStruct((b, CONV_WIDTH // LANES, t, LANES), BF16)]
    out_specs = [pl.BlockSpec((1, HEAD_PAIRS, t, LANES), lambda i: (i, 0, 0, 0)),
                 pl.BlockSpec((1, CONV_WIDTH // LANES, t, LANES), lambda i: (i, 0, 0, 0))]
    if out_states:
        out_shape += [jax.ShapeDtypeStruct((b, HEAD_PAIRS, LANES, LANES), F32)] * 2
        out_specs += [state_spec, state_spec]
    return pl.pallas_call(
        functools.partial(_mixers_kernel, t=t, has_init=has_init, out_states=out_states),
        out_shape=tuple(out_shape),
        grid=(b,),
        in_specs=in_specs,
        out_specs=tuple(out_specs),
        scratch_shapes=[
            pltpu.VMEM((HEAD_PAIRS, LANES, LANES), F32),
            pltpu.VMEM((HEAD_PAIRS, LANES, LANES), F32),
            pltpu.VMEM((n_chunks, HEAD_PAIRS, 2, LANES, LANES), BF16),
        ],
        compiler_params=_params(1),
        name="mixers",
    )(*args)


def _out_proj_kernel(x_ref, mod_ref, ret_ref, conv_ref, z_ref, cm_ref, sm_ref, w_ref, o_ref, h_ref,
                     *, d_model, dft_scale):
    nb, tt, d = x_ref.shape
    gate = mod_ref[0, :, 2 * d_model:3 * d_model]
    shift = mod_ref[0, :, 3 * d_model:4 * d_model]
    scale = mod_ref[0, :, 4 * d_model:5 * d_model]
    rb = min(tt, OUT_PROJ_ROWS)
    for n in range(nb):
        zr = z_ref[n, :, 0:FOURIER_WIDTH]
        zi = z_ref[n, :, FOURIER_WIDTH:2 * FOURIER_WIDTH]
        for r in range(tt // rb):
            rows = slice(r * rb, (r + 1) * rb)
            cm = jnp.concatenate([cm_ref[a, rows, :] for a in range(cm_ref.shape[0])], axis=1)
            sm = jnp.concatenate([sm_ref[a, rows, :] for a in range(sm_ref.shape[0])], axis=1)
            four = ((_dot(cm, zr) - _dot(sm, zi)) * dft_scale).astype(BF16)
            ret = jnp.concatenate([ret_ref[n, p, rows, :] for p in range(HEAD_PAIRS)], axis=1)
            acc = _dot(ret, w_ref[0:RET_WIDTH, :])
            acc = acc + _dot(four, w_ref[RET_WIDTH:RET_WIDTH + FOURIER_WIDTH, :])
            conv = jnp.concatenate([conv_ref[n, j, rows, :] for j in range(CONV_WIDTH // LANES)], axis=1)
            acc = acc + _dot(conv, w_ref[RET_WIDTH + FOURIER_WIDTH:, :])
            y = x_ref[n, rows, :] + gate * acc
            o_ref[n, rows, :] = y
            h_ref[n, rows, :] = _mod_norm(y, shift, scale).astype(h_ref.dtype)


def _out_proj(x, mod, ret, conv, z, cos_m, sin_m, w_out, *, nb, tt):
    b, t, d = x.shape
    shared_mod = mod.shape[0] == 1
    assert nb == 1 or (shared_mod and tt == t)
    mod_map = (lambda j, i: (0, 0, 0)) if shared_mod else (lambda j, i: (i, 0, 0))
    blk = lambda w: pl.BlockSpec((nb, tt, w), lambda j, i: (i, j, 0))
    dft_blk = pl.BlockSpec((t // LANES, tt, LANES), lambda j, i: (0, j, 0), pipeline_mode=pl.Buffered(1))
    dft_scale = float(1.0 / np.sqrt(float(t) * FOURIER_GROUP_DIM))
    return pl.pallas_call(
        functools.partial(_out_proj_kernel, d_model=d, dft_scale=dft_scale),
        out_shape=(jax.ShapeDtypeStruct(x.shape, F32), jax.ShapeDtypeStruct(x.shape, BF16)),
        grid=(t // tt, b // nb),
        in_specs=[blk(d), pl.BlockSpec((1, 1, mod.shape[2]), mod_map),
                  pl.BlockSpec((nb, HEAD_PAIRS, tt, LANES), lambda j, i: (i, 0, j, 0)),
                  pl.BlockSpec((nb, CONV_WIDTH // LANES, tt, LANES), lambda j, i: (i, 0, j, 0)),
                  pl.BlockSpec((nb, t, z.shape[2]), lambda j, i: (i, 0, 0)),
                  dft_blk, dft_blk, _resident(w_out.shape)],
        out_specs=(blk(d), blk(d)),
        compiler_params=_params(2),
        name="out_proj",
    )(x, mod, ret, conv, z, cos_m, sin_m, w_out)


def _ffn_kernel(*refs, d_model, hidden, fh, halo, final_norm):
    it = iter(refs)
    x_ref, h_ref = next(it), next(it)
    hp_ref = hn_ref = None
    if halo:
        hp_ref, hn_ref = next(it), next(it)
    mod_ref, wu_ref, cw_ref, wd_ref = (next(it) for _ in range(4))
    fg_ref = next(it) if final_norm else None
    o_ref, he_ref, act_ref = next(it), next(it), next(it)

    _, tt, d = x_ref.shape
    i = pl.program_id(1)
    he_ref[0:tt, :] = jnp.swapaxes(h_ref[0].reshape(PERM, tt // PERM, d), 0, 1).reshape(tt, d)
    if halo:
        rid = lax.broadcasted_iota(jnp.int32, (HALO, d), 0)
        before = jnp.where(i == 0, 0.0, hp_ref[0].astype(F32)[HALO - 1:HALO, :])
        after = jnp.where(i == pl.num_programs(1) - 1, 0.0, hn_ref[0].astype(F32)[0:1, :])
        edge = jnp.where(rid == 0, before, jnp.where(rid == 1, after, 0.0))
    else:
        edge = jnp.zeros((HALO, d), F32)
    he_ref[tt:tt + HALO, :] = edge.astype(BF16)

    def conv_rows(u, w):
        main = u[0:tt]
        first_prev = jnp.concatenate([u[tt:tt + 1], main[tt - PERM:tt - 1]], axis=0)
        last_next = jnp.concatenate([main[1:PERM], u[tt + 1:tt + 2]], axis=0)
        prev = jnp.concatenate([first_prev, main[0:tt - PERM]], axis=0)
        nxt = jnp.concatenate([main[PERM:tt], last_next], axis=0)
        return prev * w[0:1, :] + main * w[1:2, :] + nxt * w[2:3, :]

    for j in range(hidden // fh):
        vc = slice(j * fh, (j + 1) * fh)
        gc = slice(hidden + j * fh, hidden + (j + 1) * fh)
        he = he_ref[...]
        val = conv_rows(_dot(he, wu_ref[:, vc]), cw_ref[:, vc])
        gate = conv_rows(_dot(he, wu_ref[:, gc]), cw_ref[:, gc])
        act_ref[:, vc] = (val * (gate * jax.nn.sigmoid(gate))).astype(BF16)

    gate2 = mod_ref[0, :, 5 * d_model:6 * d_model]
    down = _dot(act_ref[...], wd_ref[...])
    y = x_ref[0] + gate2 * jnp.swapaxes(down.reshape(tt // PERM, PERM, d), 0, 1).reshape(tt, d)
    if final_norm:
        ms = jnp.mean(y * y, axis=-1, keepdims=True)
        y = (y * lax.rsqrt(ms + EPS)) * fg_ref[...]
    o_ref[0] = y


def _ffn(x, h, mod, w_up, conv_w, w_down, final_g, *, tt, fh):
    b, t, d = x.shape
    hidden = w_down.shape[0]
    assert hidden % fh == 0 and t % tt == 0 and tt % HALO == 0
    halo = tt < t
    shared_mod = mod.shape[0] == 1
    mod_map = (lambda bi, i: (0, 0, 0)) if shared_mod else (lambda bi, i: (bi, 0, 0))
    final_norm = final_g is not None
    tile = pl.BlockSpec((1, tt, d), lambda bi, i: (bi, i, 0))
    in_specs = [tile, tile]
    args = [x, h]
    if halo:
        per = tt // HALO
        last = t // HALO - 1
        in_specs += [
            pl.BlockSpec((1, HALO, d), lambda bi, i: (bi, jnp.maximum(i * per - 1, 0), 0)),
            pl.BlockSpec((1, HALO, d), lambda bi, i: (bi, jnp.minimum((i + 1) * per, last), 0)),
        ]
        args += [h, h]
    in_specs += [pl.BlockSpec((1, 1, mod.shape[2]), mod_map),
                 _resident(w_up.shape), _resident(conv_w.shape), _resident(w_down.shape)]
    args += [mod, w_up, conv_w, w_down]
    if final_norm:
        in_specs.append(_resident((1, d)))
        args.append(final_g.reshape(1, d))
    return pl.pallas_call(
        functools.partial(_ffn_kernel, d_model=d, hidden=hidden, fh=fh, halo=halo, final_norm=final_norm),
        out_shape=jax.ShapeDtypeStruct(x.shape, F32),
        grid=(b, t // tt),
        in_specs=in_specs,
        out_specs=tile,
        scratch_shapes=[pltpu.VMEM((tt + HALO, d), BF16), pltpu.VMEM((tt, hidden), BF16)],
        compiler_params=_params(2),
        name="ffn",
    )(*args)


def _rope_tables(t):
    rows = t // GRID_W
    row = jnp.repeat(jnp.arange(rows, dtype=F32), GRID_W)
    col = jnp.tile(jnp.arange(GRID_W, dtype=F32), rows)
    n_freq = HEAD_DIM // 4
    freq = ROPE_BASE ** (-jnp.arange(n_freq, dtype=F32) / n_freq)
    ang = jnp.concatenate([row[:, None] * freq, col[:, None] * freq], axis=-1)
    reps = LANES // ang.shape[1]
    cos = jnp.tile(jnp.cos(ang), (1, reps))
    sign = jnp.where((jnp.arange(LANES) % HEAD_DIM) < HEAD_DIM // 2, -1.0, 1.0).astype(F32)
    sin = jnp.tile(jnp.sin(ang), (1, reps)) * sign
    return cos, sin


def _decay_tables(decay_logit):
    ld = jax.nn.log_sigmoid(decay_logit.astype(F32))
    ld_f, ld_b = ld[0], ld[1]
    i = jnp.arange(CHUNK, dtype=F32)
    lane_f = jnp.repeat(ld_f, HEAD_DIM)[None, :]
    lane_b = jnp.repeat(ld_b, HEAD_DIM)[None, :]
    diff = i[:, None] - i[None, :]
    dmask = jnp.where(diff > 0, jnp.exp(ld_f[:, None, None] * jnp.maximum(diff, 0.0)),
                      jnp.where(diff < 0, jnp.exp(ld_b[:, None, None] * jnp.maximum(-diff, 0.0)), 2.0))
    def slabs(tab):
        return tab.reshape(CHUNK, HEAD_PAIRS, LANES).transpose(1, 0, 2)

    tables = {
        "dmask": dmask.reshape(HEAD_PAIRS, 2, CHUNK, CHUNK),
        "xi_f": jnp.exp(lane_f * (i[:, None] + 1.0)),
        "xi_b": jnp.exp(lane_b * (CHUNK - i[:, None])),
        "zeta_f": jnp.exp(lane_f * (CHUNK - 1.0 - i[:, None])),
        "zeta_b": jnp.exp(lane_b * i[:, None]),
        "cd_f": jnp.exp(lane_f * CHUNK),
        "cd_b": jnp.exp(lane_b * CHUNK),
    }
    for name in ("xi_f", "xi_b", "zeta_f", "zeta_b"):
        tables[name] = slabs(tables[name])
    return tables


def _position_dft(t):
    assert t % LANES == 0
    r = jnp.arange(t, dtype=jnp.int32)

    def table(cols):
        ang = ((cols[:, None] * r[None, :]) % t).astype(F32) * (2.0 * np.pi / t)
        return jnp.cos(ang), jnp.sin(ang)

    ca, sa = table(jnp.arange(t // LANES, dtype=jnp.int32) * LANES)
    cb, sb = table(jnp.arange(LANES, dtype=jnp.int32))
    ca, sa = ca[:, :, None], sa[:, :, None]
    cb, sb = cb.T[None, :, :], sb.T[None, :, :]
    return (ca * cb - sa * sb).astype(BF16), (sa * cb + ca * sb).astype(BF16)


def _channel_dft():
    n = FOURIER_GROUP_DIM
    idx = jnp.arange(n, dtype=jnp.int32)
    ang = ((idx[:, None] * idx[None, :]) % n).astype(F32) * (2.0 * np.pi / n)
    eye = jnp.eye(FOURIER_GROUPS, dtype=F32)
    return jnp.concatenate([jnp.kron(eye, jnp.cos(ang)), jnp.kron(eye, jnp.sin(ang))], axis=1).astype(BF16)


def kernel(x, c, ctx, c_ctx, w_mod, b_mod, w_in, ret_decay_logit, mix_conv_w, w_out,
           ffn_w_up, ffn_conv_w, ffn_w_down, final_norm_g):
    b, t, d = x.shape
    _, lc, _ = ctx.shape
    depth = w_mod.shape[0]

    rows = ((b + 1 + 7) // 8) * 8
    c_all = jnp.zeros((rows, d), F32).at[:b].set(c).at[b].set(c_ctx)
    mod_all = _modulation(c_all, w_mod, b_mod)

    rope = _rope_tables(t)
    dft_c = _channel_dft()
    dft_x = _position_dft(t)
    dft_ctx = _position_dft(lc)

    tt = min(t, 512)
    tt_big = min(t, 1024)
    nb_ctx = max(1, min(b, 1024 // lc))
    fh = 256

    for l in range(depth):
        last = l == depth - 1
        mod_x = mod_all[l, :b].reshape(b, 1, 6 * d)
        mod_c = mod_all[l, b:b + 1].reshape(1, 1, 6 * d)
        w_in_l = w_in[l].astype(BF16)
        w_out_l = w_out[l].astype(BF16)
        w_up_l = ffn_w_up[l].astype(BF16)
        w_down_l = ffn_w_down[l].astype(BF16)
        tables = _decay_tables(ret_decay_logit[l])

        if last:
            kz_c, v_c = _in_proj(ctx, mod_c, w_in_l, tables, None, dft_c, nb=nb_ctx, tt=lc, kv_only=True)
            st_f, st_b = _retention_states(kz_c, v_c, tables)
        else:
            *qkvgz_c, z_c, pc_c = _in_proj(ctx, mod_c, w_in_l, tables, None, dft_c, nb=nb_ctx, tt=lc)
            ret_c, conv_c, st_f, st_b = _mixers(*qkvgz_c, pc_c, tables, mix_conv_w[l], None, out_states=True)

        *qkvgz, z, pc = _in_proj(x, mod_x, w_in_l, tables, rope, dft_c, nb=1, tt=tt_big)
        ret, conv = _mixers(*qkvgz, pc, tables, mix_conv_w[l], (st_f, st_b), out_states=False)
        x, h = _out_proj(x, mod_x, ret, conv, z, *dft_x, w_out_l, nb=1, tt=tt_big)
        x = _ffn(x, h, mod_x, w_up_l, ffn_conv_w[l], w_down_l, final_norm_g if last else None, tt=tt_big, fh=fh)

        if not last:
            ctx, h_c = _out_proj(ctx, mod_c, ret_c, conv_c, z_c, *dft_ctx, w_out_l, nb=nb_ctx, tt=lc)
            ctx = _ffn(ctx, h_c, mod_c, w_up_l, ffn_conv_w[l], w_down_l, None, tt=lc, fh=fh)
    return x
```

```python
import functools

import jax
import jax.numpy as jnp
import numpy as np
from jax import lax
from jax.experimental import pallas as pl
from jax.experimental.pallas import tpu as pltpu

F32 = jnp.float32
BF16 = jnp.bfloat16

GRID_W = 64
RET_HEADS = 8
HEAD_DIM = 64
RET_WIDTH = RET_HEADS * HEAD_DIM
FOURIER_GROUPS = 4
FOURIER_GROUP_DIM = 64
FOURIER_WIDTH = FOURIER_GROUPS * FOURIER_GROUP_DIM
CONV_WIDTH = 256
ROPE_BASE = 10000.0
EPS = 1e-6
CHUNK = 128

LANES = 128
BF16_SUBLANES = 16
V7X_VMEM_LIMIT_BYTES = 60000 * 1024

HEAD_PAIRS = RET_WIDTH // LANES
HALO = BF16_SUBLANES
OUT_PROJ_ROWS = 512
PERM = 8

Q0 = 0
K0 = Q0 + RET_WIDTH
V0 = K0 + RET_WIDTH
G0 = V0 + RET_WIDTH
F0 = G0 + RET_WIDTH
CB0 = F0 + FOURIER_WIDTH
IN_WIDTH = CB0 + 3 * CONV_WIDTH


def _params(n_grid_dims, vmem=V7X_VMEM_LIMIT_BYTES):
    return pltpu.CompilerParams(
        dimension_semantics=("arbitrary",) * n_grid_dims,
        vmem_limit_bytes=vmem,
    )


def _dot(a, b):
    return jnp.dot(a, b, preferred_element_type=F32)


def _mod_norm(x, shift, scale):
    ms = jnp.mean(x * x, axis=-1, keepdims=True)
    return (x * lax.rsqrt(ms + EPS)) * (1.0 + scale) + shift


def _resident(shape):
    return pl.BlockSpec(shape, lambda *_: (0,) * len(shape), pipeline_mode=pl.Buffered(1))


def _mod_kernel(c_ref, w_ref, b_ref, o_ref):
    c = c_ref[...]
    s = c * jax.nn.sigmoid(c)
    o_ref[0] = jnp.dot(s, w_ref[0], preferred_element_type=F32,
                       precision=lax.Precision.HIGHEST) + b_ref[0]


def _modulation(c_all, w_mod, b_mod):
    depth, d, n = w_mod.shape
    rows = c_all.shape[0]
    tn = 1024
    return pl.pallas_call(
        _mod_kernel,
        out_shape=jax.ShapeDtypeStruct((depth, rows, n), F32),
        grid=(depth, n // tn),
        in_specs=[
            pl.BlockSpec((rows, d), lambda l, j: (0, 0)),
            pl.BlockSpec((1, d, tn), lambda l, j: (l, 0, j)),
            pl.BlockSpec((1, 1, tn), lambda l, j: (l, 0, j)),
        ],
        out_specs=pl.BlockSpec((1, rows, tn), lambda l, j: (l, 0, j)),
        compiler_params=_params(2),
        name="modulation",
    )(c_all, w_mod, b_mod.reshape(depth, 1, n))


def _rope_block(t, cos, sin, first_half):
    lo = pltpu.roll(t, LANES - HEAD_DIM // 2, 1)
    hi = pltpu.roll(t, HEAD_DIM // 2, 1)
    return t * cos + jnp.where(first_half, lo, hi) * sin


def _in_proj_kernel(*refs, d_model, use_rope, kv_only):
    it = iter(refs)
    x_ref, mod_ref, w_ref, zf_ref, zb_ref = (next(it) for _ in range(5))
    cos_ref = sin_ref = dft_ref = None
    if use_rope:
        cos_ref, sin_ref = next(it), next(it)
    if not kv_only:
        dft_ref = next(it)
    outs = list(it)

    nb, tt, d = x_ref.shape
    m = nb * tt
    x = x_ref[...].reshape(m, d)
    shift = mod_ref[0, :, 0:d_model]
    scale = mod_ref[0, :, d_model:2 * d_model]
    h = _mod_norm(x, shift, scale).astype(BF16)

    def put(ref, col, val):
        ref[:, :, col:col + val.shape[1]] = val.astype(ref.dtype).reshape(nb, tt, val.shape[1])

    def put_block(ref, j, val):
        ref[:, j, :, :] = val.astype(ref.dtype).reshape(nb, tt, LANES)

    if use_rope:
        cos = cos_ref[...]
        sin = sin_ref[...]
        lane = lax.broadcasted_iota(jnp.int32, (m, LANES), 1)
        first_half = (lane % HEAD_DIM) < (HEAD_DIM // 2)

    def rotary(t, j):
        tb = t[:, j * LANES:(j + 1) * LANES]
        return _rope_block(tb, cos, sin, first_half) if use_rope else tb

    def chunk_tiled(ref, j):
        return jnp.concatenate([ref[j]] * (m // CHUNK), axis=0)

    kz_ref, v_ref = (outs[0], outs[1]) if kv_only else (outs[4], outs[2])
    k = _dot(h, w_ref[:, K0:V0])
    v = _dot(h, w_ref[:, V0:G0])
    for j in range(HEAD_PAIRS):
        kb = rotary(k, j) * (HEAD_DIM ** -0.5)
        if not kv_only:
            put_block(outs[1], j, kb)
        put_block(kz_ref, j, kb * chunk_tiled(zf_ref, j))
        put_block(kz_ref, HEAD_PAIRS + j, kb * chunk_tiled(zb_ref, j))
        put_block(v_ref, j, v[:, j * LANES:(j + 1) * LANES])
    if kv_only:
        return

    q_ref, _, _, g_ref, _, z_ref, pc_ref = outs
    q = _dot(h, w_ref[:, Q0:K0])
    g = _dot(h, w_ref[:, G0:F0])
    g = g * jax.nn.sigmoid(g)
    for j in range(HEAD_PAIRS):
        put_block(q_ref, j, rotary(q, j))
        put_block(g_ref, j, g[:, j * LANES:(j + 1) * LANES])
    fpc = _dot(h, w_ref[:, F0:IN_WIDTH])
    f = fpc[:, 0:FOURIER_WIDTH].astype(BF16)
    put(z_ref, 0, _dot(f, dft_ref[...]))
    cb = fpc[:, CB0 - F0:CB0 - F0 + CONV_WIDTH]
    gp = fpc[:, CB0 - F0 + CONV_WIDTH:CB0 - F0 + 2 * CONV_WIDTH] * fpc[:, CB0 - F0 + 2 * CONV_WIDTH:]
    for j in range(CONV_WIDTH // LANES):
        put_block(pc_ref, j, cb[:, j * LANES:(j + 1) * LANES])
        put_block(pc_ref, CONV_WIDTH // LANES + j, gp[:, j * LANES:(j + 1) * LANES])


def _in_proj(x, mod, w_in, tables, rope, dft_c, *, nb, tt, kv_only=False):
    b, t, d = x.shape
    use_rope = rope is not None
    shared_mod = mod.shape[0] == 1
    assert nb == 1 or (shared_mod and not use_rope)
    assert tt % CHUNK == 0
    mod_map = (lambda i, j: (0, 0, 0)) if shared_mod else (lambda i, j: (i, 0, 0))
    in_specs = [
        pl.BlockSpec((nb, tt, d), lambda i, j: (i, j, 0)),
        pl.BlockSpec((1, 1, mod.shape[2]), mod_map),
        _resident(w_in.shape),
        _resident(tables["zeta_f"].shape),
        _resident(tables["zeta_b"].shape),
    ]
    args = [x, mod, w_in, tables["zeta_f"], tables["zeta_b"]]
    if use_rope:
        in_specs += [pl.BlockSpec((tt, LANES), lambda i, j: (j, 0))] * 2
        args += list(rope)
    if kv_only:
        widths = (2 * RET_WIDTH, RET_WIDTH)
        blocked = (True, True)
    else:
        in_specs.append(_resident(dft_c.shape))
        args.append(dft_c)
        widths = (RET_WIDTH,) * 4 + (2 * RET_WIDTH, 2 * FOURIER_WIDTH, 2 * CONV_WIDTH)
        blocked = (True,) * 5 + (False, True)
    out_shape = tuple(jax.ShapeDtypeStruct((b, w // LANES, t, LANES) if blk else (b, t, w), BF16)
                      for w, blk in zip(widths, blocked))
    out_specs = tuple(pl.BlockSpec((nb, w // LANES, tt, LANES), lambda i, j: (i, 0, j, 0)) if blk
                      else pl.BlockSpec((nb, tt, w), lambda i, j: (i, j, 0))
                      for w, blk in zip(widths, blocked))
    return pl.pallas_call(
        functools.partial(_in_proj_kernel, d_model=d, use_rope=use_rope, kv_only=kv_only),
        out_shape=out_shape,
        grid=(b // nb, t // tt),
        in_specs=in_specs,
        out_specs=out_specs,
        compiler_params=_params(2),
        name="in_proj_kv" if kv_only else "in_proj",
    )(*args)


def _block_diag_mask():
    row = lax.broadcasted_iota(jnp.int32, (LANES, LANES), 0)
    col = lax.broadcasted_iota(jnp.int32, (LANES, LANES), 1)
    return (row < HEAD_DIM) == (col < HEAD_DIM)


def _state_sweep(kz_ref, v_ref, cdf_ref, cdb_ref, sf_ref, sb_ref, s_all, n_chunks):
    diag = _block_diag_mask()

    def update(state_ref, p, kz, v, decay):
        u = lax.dot_general(kz, v, (((0,), (0,)), ((), ())), preferred_element_type=F32)
        state_ref[p] = decay * state_ref[p] + jnp.where(diag, u, 0.0)

    def body(i, carry):
        rf = pl.multiple_of(i * CHUNK, CHUNK)
        nb_ = n_chunks - 1 - i
        rb = pl.multiple_of(nb_ * CHUNK, CHUNK)
        for p in range(HEAD_PAIRS):
            lanes = slice(p * LANES, (p + 1) * LANES)
            if s_all is not None:
                s_all[i, p, 0] = sf_ref[p].astype(BF16)
                s_all[nb_, p, 1] = sb_ref[p].astype(BF16)
            update(sf_ref, p, kz_ref[0, p, pl.ds(rf, CHUNK), :], v_ref[0, p, pl.ds(rf, CHUNK), :],
                   cdf_ref[:, lanes])
            update(sb_ref, p, kz_ref[0, HEAD_PAIRS + p, pl.ds(rb, CHUNK), :], v_ref[0, p, pl.ds(rb, CHUNK), :],
                   cdb_ref[:, lanes])
        return carry

    lax.fori_loop(0, n_chunks, body, 0, unroll=8 if n_chunks % 8 == 0 else (2 if n_chunks % 2 == 0 else 1))


def _retention_states_kernel(kz_ref, v_ref, cdf_ref, cdb_ref, sf_out, sb_out, *, t):
    sf_out[...] = jnp.zeros(sf_out.shape, F32)
    sb_out[...] = jnp.zeros(sb_out.shape, F32)
    _state_sweep(kz_ref, v_ref, cdf_ref, cdb_ref, sf_out.at[0], sb_out.at[0], None, t // CHUNK)


def _blocked_spec(a):
    return pl.BlockSpec((1,) + a.shape[1:], lambda i: (i, 0, 0, 0))


def _retention_states(kz, v, tables):
    b, _, t, _ = v.shape
    state = jax.ShapeDtypeStruct((b, HEAD_PAIRS, LANES, LANES), F32)
    ins = [tables["cd_f"], tables["cd_b"]]
    return pl.pallas_call(
        functools.partial(_retention_states_kernel, t=t),
        out_shape=(state, state),
        grid=(b,),
        in_specs=[_blocked_spec(kz), _blocked_spec(v)] + [_resident(a.shape) for a in ins],
        out_specs=(pl.BlockSpec((1, HEAD_PAIRS, LANES, LANES), lambda i: (i, 0, 0, 0)),) * 2,
        compiler_params=_params(1),
        name="retention_states",
    )(kz, v, *ins)


def _mixers_kernel(*refs, t, has_init, out_states):
    it = iter(refs)
    q_ref, k_ref, v_ref, g_ref, kz_ref, pc_ref = (next(it) for _ in range(6))
    dmask_ref, xif_ref, xib_ref, cdf_ref, cdb_ref, cw_ref = (next(it) for _ in range(6))
    if has_init:
        if_ref, ib_ref = next(it), next(it)
    ret_ref, conv_ref = next(it), next(it)
    if out_states:
        sf_out, sb_out = next(it), next(it)
    sf_ref, sb_ref, s_all = next(it), next(it), next(it)

    n_chunks = t // CHUNK
    if has_init:
        sf_ref[...] = if_ref[0]
        sb_ref[...] = ib_ref[0]
    else:
        sf_ref[...] = jnp.zeros(sf_ref.shape, F32)
        sb_ref[...] = jnp.zeros(sb_ref.shape, F32)

    _state_sweep(kz_ref, v_ref, cdf_ref, cdb_ref, sf_ref, sb_ref, s_all, n_chunks)
    if out_states:
        sf_out[0] = sf_ref[...]
        sb_out[0] = sb_ref[...]

    lane = lax.broadcasted_iota(jnp.int32, (1, LANES), 1)
    head0 = lane < HEAD_DIM
    m0 = jnp.where(head0, 1.0, 0.0).astype(BF16)
    m1 = jnp.where(head0, 0.0, 1.0).astype(BF16)
    w = cw_ref[...]
    a0 = jnp.where(head0, 1.0 / HEAD_DIM, 0.0)
    a1 = jnp.where(head0, 0.0, 1.0 / HEAD_DIM)

    def out_chunk(n, carry):
        r0 = pl.multiple_of(n * CHUNK, CHUNK)
        rows = pl.ds(r0, CHUNK)
        for p in range(HEAD_PAIRS):
            lanes = slice(p * LANES, (p + 1) * LANES)
            q = q_ref[0, p, rows, :]
            k = k_ref[0, p, rows, :]
            v = v_ref[0, p, rows, :]
            kk = jnp.concatenate([k * m0, k * m1], axis=0)
            vv = jnp.concatenate([v * m0, v * m1], axis=0)
            s = lax.dot_general(q, kk, (((1,), (1,)), ((), ())), preferred_element_type=F32)
            dmask = jnp.concatenate([dmask_ref[p, 0], dmask_ref[p, 1]], axis=1)
            pr = (s * dmask).astype(BF16)
            o = _dot(pr, vv)
            c = _dot(q, jnp.concatenate([s_all[n, p, 0], s_all[n, p, 1]], axis=1))
            o = o + c[:, 0:LANES] * xif_ref[p] + c[:, LANES:2 * LANES] * xib_ref[p]
            o2 = o * o + EPS
            r0n = lax.rsqrt(jnp.sum(o2 * a0, axis=-1, keepdims=True))
            r1n = lax.rsqrt(jnp.sum(o2 * a1, axis=-1, keepdims=True))
            o = o * jnp.where(head0, r0n, r1n)
            ret_ref[0, p, rows, :] = (o * g_ref[0, p, rows, :].astype(F32)).astype(ret_ref.dtype)

        def conv_rows(first, start, size):
            return jnp.concatenate([pc_ref[0, first + j, pl.ds(start, size), :]
                                    for j in range(CONV_WIDTH // LANES)], axis=1).astype(F32)

        def gate_prod(start, size):
            return conv_rows(CONV_WIDTH // LANES, start, size)

        lo = pl.multiple_of(jnp.maximum(r0 - HALO, 0), HALO)
        hi = pl.multiple_of(jnp.minimum(r0 + CHUNK, t - HALO), HALO)
        before = jnp.where(n == 0, 0.0, gate_prod(lo, HALO))
        after = jnp.where(n == n_chunks - 1, 0.0, gate_prod(hi, HALO))
        mid = gate_prod(r0, CHUNK)
        ext = jnp.concatenate([before, mid, after], axis=0)
        acc = (ext[HALO - 1:HALO - 1 + CHUNK] * w[0:1, :] + mid * w[1:2, :]
               + ext[HALO + 1:HALO + 1 + CHUNK] * w[2:3, :])
        conv = (conv_rows(0, r0, CHUNK) * acc).astype(conv_ref.dtype)
        for j in range(CONV_WIDTH // LANES):
            conv_ref[0, j, rows, :] = conv[:, j * LANES:(j + 1) * LANES]
        return carry

    lax.fori_loop(0, n_chunks, out_chunk, 0, unroll=8 if n_chunks % 8 == 0 else 2)


def _mixers(q, k, v, g, kz, pc, tables, conv_w, init, *, out_states):
    b, _, t, _ = q.shape
    has_init = init is not None
    n_chunks = t // CHUNK
    tabs = [tables[k] for k in ("dmask", "xi_f", "xi_b", "cd_f", "cd_b")] + [conv_w]
    args = [q, k, v, g, kz, pc]
    in_specs = [_blocked_spec(a) for a in args] + [_resident(a.shape) for a in tabs]
    args = args + tabs
    state_spec = pl.BlockSpec((1, HEAD_PAIRS, LANES, LANES), lambda i: (i, 0, 0, 0))
    if has_init:
        in_specs += [state_spec, state_spec]
        args += list(init)
    out_shape = [jax.ShapeDtypeStruct((b, HEAD_PAIRS, t, LANES), BF16),
                 jax.ShapeDtypeStruct((b, CONV_WIDTH // LANES, t, LANES), BF16)]
    out_specs = [pl.BlockSpec((1, HEAD_PAIRS, t, LANES), lambda i: (i, 0, 0, 0)),
                 pl.BlockSpec((1, CONV_WIDTH // LANES, t, LANES), lambda i: (i, 0, 0, 0))]
    if out_states:
        out_shape += [jax.ShapeDtypeStruct((b, HEAD_PAIRS, LANES, LANES), F32)] * 2
        out_specs += [state_spec, state_spec]
    return pl.pallas_call(
        functools.partial(_mixers_kernel, t=t, has_init=has_init, out_states=out_states),
        out_shape=tuple(out_shape),
        grid=(b,),
        in_specs=in_specs,
        out_specs=tuple(out_specs),
        scratch_shapes=[
            pltpu.VMEM((HEAD_PAIRS, LANES, LANES), F32),
            pltpu.VMEM((HEAD_PAIRS, LANES, LANES), F32),
            pltpu.VMEM((n_chunks, HEAD_PAIRS, 2, LANES, LANES), BF16),
        ],
        compiler_params=_params(1),
        name="mixers",
    )(*args)


def _out_proj_kernel(x_ref, mod_ref, ret_ref, conv_ref, z_ref, cm_ref, sm_ref, w_ref, o_ref, h_ref,
                     *, d_model, dft_scale):
    nb, tt, d = x_ref.shape
    gate = mod_ref[0, :, 2 * d_model:3 * d_model]
    shift = mod_ref[0, :, 3 * d_model:4 * d_model]
    scale = mod_ref[0, :, 4 * d_model:5 * d_model]
    rb = min(tt, OUT_PROJ_ROWS)
    for n in range(nb):
        zr = z_ref[n, :, 0:FOURIER_WIDTH]
        zi = z_ref[n, :, FOURIER_WIDTH:2 * FOURIER_WIDTH]
        for r in range(tt // rb):
            rows = slice(r * rb, (r + 1) * rb)
            trow = pl.ds(pl.multiple_of(pl.program_id(1) * tt + r * rb, rb), rb)
            cm = jnp.concatenate([cm_ref[a, trow, :] for a in range(cm_ref.shape[0])], axis=1)
            sm = jnp.concatenate([sm_ref[a, trow, :] for a in range(sm_ref.shape[0])], axis=1)
            four = ((_dot(cm, zr) - _dot(sm, zi)) * dft_scale).astype(BF16)
            ret = jnp.concatenate([ret_ref[n, p, rows, :] for p in range(HEAD_PAIRS)], axis=1)
            acc = _dot(ret, w_ref[0:RET_WIDTH, :])
            acc = acc + _dot(four, w_ref[RET_WIDTH:RET_WIDTH + FOURIER_WIDTH, :])
            conv = jnp.concatenate([conv_ref[n, j, rows, :] for j in range(CONV_WIDTH // LANES)], axis=1)
            acc = acc + _dot(conv, w_ref[RET_WIDTH + FOURIER_WIDTH:, :])
            y = x_ref[n, rows, :] + gate * acc
            o_ref[n, rows, :] = y
            h_ref[n, rows, :] = _mod_norm(y, shift, scale).astype(h_ref.dtype)


def _out_proj(x, mod, ret, conv, z, cos_m, sin_m, w_out, *, nb, tt):
    b, t, d = x.shape
    shared_mod = mod.shape[0] == 1
    assert nb == 1 or (shared_mod and tt == t)
    mod_map = (lambda i, j: (0, 0, 0)) if shared_mod else (lambda i, j: (i, 0, 0))
    blk = lambda w: pl.BlockSpec((nb, tt, w), lambda i, j: (i, j, 0))
    dft_blk = _resident((t // LANES, t, LANES))
    dft_scale = float(1.0 / np.sqrt(float(t) * FOURIER_GROUP_DIM))
    return pl.pallas_call(
        functools.partial(_out_proj_kernel, d_model=d, dft_scale=dft_scale),
        out_shape=(jax.ShapeDtypeStruct(x.shape, F32), jax.ShapeDtypeStruct(x.shape, BF16)),
        grid=(b // nb, t // tt),
        in_specs=[blk(d), pl.BlockSpec((1, 1, mod.shape[2]), mod_map),
                  pl.BlockSpec((nb, HEAD_PAIRS, tt, LANES), lambda i, j: (i, 0, j, 0)),
                  pl.BlockSpec((nb, CONV_WIDTH // LANES, tt, LANES), lambda i, j: (i, 0, j, 0)),
                  pl.BlockSpec((nb, t, z.shape[2]), lambda i, j: (i, 0, 0)),
                  dft_blk, dft_blk, _resident(w_out.shape)],
        out_specs=(blk(d), blk(d)),
        compiler_params=_params(2),
        name="out_proj",
    )(x, mod, ret, conv, z, cos_m, sin_m, w_out)


def _ffn_kernel(*refs, d_model, hidden, fh, halo, final_norm):
    it = iter(refs)
    x_ref, h_ref = next(it), next(it)
    hp_ref = hn_ref = None
    if halo:
        hp_ref, hn_ref = next(it), next(it)
    mod_ref, wu_ref, cw_ref, wd_ref = (next(it) for _ in range(4))
    fg_ref = next(it) if final_norm else None
    o_ref, he_ref, act_ref = next(it), next(it), next(it)

    _, tt, d = x_ref.shape
    i = pl.program_id(1)
    he_ref[0:tt, :] = jnp.swapaxes(h_ref[0].reshape(PERM, tt // PERM, d), 0, 1).reshape(tt, d)
    if halo:
        rid = lax.broadcasted_iota(jnp.int32, (HALO, d), 0)
        before = jnp.where(i == 0, 0.0, hp_ref[0].astype(F32)[HALO - 1:HALO, :])
        after = jnp.where(i == pl.num_programs(1) - 1, 0.0, hn_ref[0].astype(F32)[0:1, :])
        edge = jnp.where(rid == 0, before, jnp.where(rid == 1, after, 0.0))
    else:
        edge = jnp.zeros((HALO, d), F32)
    he_ref[tt:tt + HALO, :] = edge.astype(BF16)

    def conv_rows(u, w):
        main = u[0:tt]
        first_prev = jnp.concatenate([u[tt:tt + 1], main[tt - PERM:tt - 1]], axis=0)
        last_next = jnp.concatenate([main[1:PERM], u[tt + 1:tt + 2]], axis=0)
        prev = jnp.concatenate([first_prev, main[0:tt - PERM]], axis=0)
        nxt = jnp.concatenate([main[PERM:tt], last_next], axis=0)
        return prev * w[0:1, :] + main * w[1:2, :] + nxt * w[2:3, :]

    for j in range(hidden // fh):
        vc = slice(j * fh, (j + 1) * fh)
        gc = slice(hidden + j * fh, hidden + (j + 1) * fh)
        he = he_ref[...]
        val = conv_rows(_dot(he, wu_ref[:, vc]), cw_ref[:, vc])
        gate = conv_rows(_dot(he, wu_ref[:, gc]), cw_ref[:, gc])
        act_ref[:, vc] = (val * (gate * jax.nn.sigmoid(gate))).astype(BF16)

    gate2 = mod_ref[0, :, 5 * d_model:6 * d_model]
    down = _dot(act_ref[...], wd_ref[...])
    y = x_ref[0] + gate2 * jnp.swapaxes(down.reshape(tt // PERM, PERM, d), 0, 1).reshape(tt, d)
    if final_norm:
        ms = jnp.mean(y * y, axis=-1, keepdims=True)
        y = (y * lax.rsqrt(ms + EPS)) * fg_ref[...]
    o_ref[0] = y


def _ffn(x, h, mod, w_up, conv_w, w_down, final_g, *, tt, fh):
    b, t, d = x.shape
    hidden = w_down.shape[0]
    assert hidden % fh == 0 and t % tt == 0 and tt % HALO == 0
    halo = tt < t
    shared_mod = mod.shape[0] == 1
    mod_map = (lambda bi, i: (0, 0, 0)) if shared_mod else (lambda bi, i: (bi, 0, 0))
    final_norm = final_g is not None
    tile = pl.BlockSpec((1, tt, d), lambda bi, i: (bi, i, 0))
    in_specs = [tile, tile]
    args = [x, h]
    if halo:
        per = tt // HALO
        last = t // HALO - 1
        in_specs += [
            pl.BlockSpec((1, HALO, d), lambda bi, i: (bi, jnp.maximum(i * per - 1, 0), 0)),
            pl.BlockSpec((1, HALO, d), lambda bi, i: (bi, jnp.minimum((i + 1) * per, last), 0)),
        ]
        args += [h, h]
    in_specs += [pl.BlockSpec((1, 1, mod.shape[2]), mod_map),
                 _resident(w_up.shape), _resident(conv_w.shape), _resident(w_down.shape)]
    args += [mod, w_up, conv_w, w_down]
    if final_norm:
        in_specs.append(_resident((1, d)))
        args.append(final_g.reshape(1, d))
    return pl.pallas_call(
        functools.partial(_ffn_kernel, d_model=d, hidden=hidden, fh=fh, halo=halo, final_norm=final_norm),
        out_shape=jax.ShapeDtypeStruct(x.shape, F32),
        grid=(b, t // tt),
        in_specs=in_specs,
        out_specs=tile,
        scratch_shapes=[pltpu.VMEM((tt + HALO, d), BF16), pltpu.VMEM((tt, hidden), BF16)],
        compiler_params=_params(2),
        name="ffn",
    )(*args)


def _rope_tables(t):
    rows = t // GRID_W
    row = jnp.repeat(jnp.arange(rows, dtype=F32), GRID_W)
    col = jnp.tile(jnp.arange(GRID_W, dtype=F32), rows)
    n_freq = HEAD_DIM // 4
    freq = ROPE_BASE ** (-jnp.arange(n_freq, dtype=F32) / n_freq)
    ang = jnp.concatenate([row[:, None] * freq, col[:, None] * freq], axis=-1)
    reps = LANES // ang.shape[1]
    cos = jnp.tile(jnp.cos(ang), (1, reps))
    sign = jnp.where((jnp.arange(LANES) % HEAD_DIM) < HEAD_DIM // 2, -1.0, 1.0).astype(F32)
    sin = jnp.tile(jnp.sin(ang), (1, reps)) * sign
    return cos, sin


def _decay_tables(decay_logit):
    ld = jax.nn.log_sigmoid(decay_logit.astype(F32))
    ld_f, ld_b = ld[0], ld[1]
    i = jnp.arange(CHUNK, dtype=F32)
    lane_f = jnp.repeat(ld_f, HEAD_DIM)[None, :]
    lane_b = jnp.repeat(ld_b, HEAD_DIM)[None, :]
    diff = i[:, None] - i[None, :]
    dmask = jnp.where(diff > 0, jnp.exp(ld_f[:, None, None] * jnp.maximum(diff, 0.0)),
                      jnp.where(diff < 0, jnp.exp(ld_b[:, None, None] * jnp.maximum(-diff, 0.0)), 2.0))
    def slabs(tab):
        return tab.reshape(CHUNK, HEAD_PAIRS, LANES).transpose(1, 0, 2)

    tables = {
        "dmask": dmask.reshape(HEAD_PAIRS, 2, CHUNK, CHUNK),
        "xi_f": jnp.exp(lane_f * (i[:, None] + 1.0)),
        "xi_b": jnp.exp(lane_b * (CHUNK - i[:, None])),
        "zeta_f": jnp.exp(lane_f * (CHUNK - 1.0 - i[:, None])),
        "zeta_b": jnp.exp(lane_b * i[:, None]),
        "cd_f": jnp.exp(lane_f * CHUNK),
        "cd_b": jnp.exp(lane_b * CHUNK),
    }
    for name in ("xi_f", "xi_b", "zeta_f", "zeta_b"):
        tables[name] = slabs(tables[name])
    return tables


def _position_dft(t):
    assert t % LANES == 0
    r = jnp.arange(t, dtype=jnp.int32)

    def table(cols):
        ang = ((cols[:, None] * r[None, :]) % t).astype(F32) * (2.0 * np.pi / t)
        return jnp.cos(ang), jnp.sin(ang)

    ca, sa = table(jnp.arange(t // LANES, dtype=jnp.int32) * LANES)
    cb, sb = table(jnp.arange(LANES, dtype=jnp.int32))
    ca, sa = ca[:, :, None], sa[:, :, None]
    cb, sb = cb.T[None, :, :], sb.T[None, :, :]
    return (ca * cb - sa * sb).astype(BF16), (sa * cb + ca * sb).astype(BF16)


def _channel_dft():
    n = FOURIER_GROUP_DIM
    idx = jnp.arange(n, dtype=jnp.int32)
    ang = ((idx[:, None] * idx[None, :]) % n).astype(F32) * (2.0 * np.pi / n)
    eye = jnp.eye(FOURIER_GROUPS, dtype=F32)
    return jnp.concatenate([jnp.kron(eye, jnp.cos(ang)), jnp.kron(eye, jnp.sin(ang))], axis=1).astype(BF16)


def kernel(x, c, ctx, c_ctx, w_mod, b_mod, w_in, ret_decay_logit, mix_conv_w, w_out,
           ffn_w_up, ffn_conv_w, ffn_w_down, final_norm_g):
    b, t, d = x.shape
    _, lc, _ = ctx.shape
    depth = w_mod.shape[0]

    rows = ((b + 1 + 7) // 8) * 8
    c_all = jnp.zeros((rows, d), F32).at[:b].set(c).at[b].set(c_ctx)
    mod_all = _modulation(c_all, w_mod, b_mod)

    rope = _rope_tables(t)
    dft_c = _channel_dft()
    dft_x = _position_dft(t)
    dft_ctx = _position_dft(lc)

    tt = min(t, 512)
    tt_big = min(t, 1024)
    nb_ctx = max(1, min(b, 1024 // lc))
    fh = 256

    for l in range(depth):
        last = l == depth - 1
        mod_x = mod_all[l, :b].reshape(b, 1, 6 * d)
        mod_c = mod_all[l, b:b + 1].reshape(1, 1, 6 * d)
        w_in_l = w_in[l].astype(BF16)
        w_out_l = w_out[l].astype(BF16)
        w_up_l = ffn_w_up[l].astype(BF16)
        w_down_l = ffn_w_down[l].astype(BF16)
        tables = _decay_tables(ret_decay_logit[l])

        if last:
            kz_c, v_c = _in_proj(ctx, mod_c, w_in_l, tables, None, dft_c, nb=nb_ctx, tt=lc, kv_only=True)
            st_f, st_b = _retention_states(kz_c, v_c, tables)
        else:
            *qkvgz_c, z_c, pc_c = _in_proj(ctx, mod_c, w_in_l, tables, None, dft_c, nb=nb_ctx, tt=lc)
            ret_c, conv_c, st_f, st_b = _mixers(*qkvgz_c, pc_c, tables, mix_conv_w[l], None, out_states=True)

        *qkvgz, z, pc = _in_proj(x, mod_x, w_in_l, tables, rope, dft_c, nb=1, tt=tt_big)
        ret, conv = _mixers(*qkvgz, pc, tables, mix_conv_w[l], (st_f, st_b), out_states=False)
        x, h = _out_proj(x, mod_x, ret, conv, z, *dft_x, w_out_l, nb=1, tt=tt_big)
        x = _ffn(x, h, mod_x, w_up_l, ffn_conv_w[l], w_down_l, final_norm_g if last else None, tt=tt_big, fh=fh)

        if not last:
            ctx, h_c = _out_proj(ctx, mod_c, ret_c, conv_c, z_c, *dft_ctx, w_out_l, nb=nb_ctx, tt=lc)
            ctx = _ffn(ctx, h_c, mod_c, w_up_l, ffn_conv_w[l], w_down_l, None, tt=lc, fh=fh)
    return x
```
